```python
import math
import jax, jax.numpy as jnp
from jax import lax
import numpy as np

D_MODEL = 1024
BATCH = 8
SEQ = 4096
DEPTH = 4

CHUNK = 64
N_MIXERS = 2
N_SSM_LAYERS = (DEPTH + 1) // 2
N_CONV_LAYERS = DEPTH // 2
SSM_GROUP = 16
SSM_GROUPS = D_MODEL // SSM_GROUP
SSM_STATE = 64
DT_MIN = 0.001
DT_MAX = 0.1
CONV_WIDTH = 3
FFN_HIDDEN = ((8 * D_MODEL // 3 + 255) // 256) * 256
PLE_DIM = 256
EPS = 1e-6

kernel_name = "hybrid_s5_shortconv_streaming_trunk"


def rmsnorm(x, g):
    xf = x.astype(jnp.float32)
    y = xf * lax.rsqrt(jnp.mean(xf * xf, axis=-1, keepdims=True) + EPS)
    return (y * g.astype(jnp.float32)).astype(x.dtype)


def _scan_combine(earlier, later):
    ae_re, ae_im, be_re, be_im = earlier
    al_re, al_im, bl_re, bl_im = later
    a_re = al_re * ae_re - al_im * ae_im
    a_im = al_re * ae_im + al_im * ae_re
    b_re = al_re * be_re - al_im * be_im + bl_re
    b_im = al_re * be_im + al_im * be_re + bl_im
    return (a_re, a_im, b_re, b_im)


def s5_mixer(u, a_re, a_im, log_dt, b_re, b_im, c_re, c_im, d_skip, w_glu):
    bsz, seq, dm = u.shape
    f32 = jnp.float32
    uf = u.astype(f32)
    a_re = a_re.astype(f32); a_im = a_im.astype(f32)
    dt = jnp.exp(log_dt.astype(f32))[:, None]
    mag = jnp.exp(dt * a_re)
    ang = dt * a_im
    abar_re = mag * jnp.cos(ang)
    abar_im = mag * jnp.sin(ang)
    nr = abar_re - 1.0
    ni = abar_im
    den = a_re * a_re + a_im * a_im
    f_re = (nr * a_re + ni * a_im) / den
    f_im = (ni * a_re - nr * a_im) / den
    b_re = b_re.astype(f32); b_im = b_im.astype(f32)
    bb_re = f_re[..., None] * b_re - f_im[..., None] * b_im
    bb_im = f_re[..., None] * b_im + f_im[..., None] * b_re
    c_re = c_re.astype(f32); c_im = c_im.astype(f32)

    n_chunks = seq // CHUNK
    u_chunks = uf.reshape(bsz, n_chunks, CHUNK, SSM_GROUPS, SSM_GROUP).transpose(1, 0, 2, 3, 4)
    a_seq_re = jnp.broadcast_to(abar_re, (1, CHUNK, SSM_GROUPS, SSM_STATE))
    a_seq_im = jnp.broadcast_to(abar_im, (1, CHUNK, SSM_GROUPS, SSM_STATE))

    def step(carry, u_c):
        h_re, h_im = carry
        bu_re = jnp.einsum('bcgk,gpk->bcgp', u_c, bb_re)
        bu_im = jnp.einsum('bcgk,gpk->bcgp', u_c, bb_im)
        acum_re, acum_im, hl_re, hl_im = lax.associative_scan(
            _scan_combine, (a_seq_re, a_seq_im, bu_re, bu_im), axis=1)
        hs_re = hl_re + acum_re * h_re[:, None] - acum_im * h_im[:, None]
        hs_im = hl_im + acum_re * h_im[:, None] + acum_im * h_re[:, None]
        y = (jnp.einsum('bcgp,gkp->bcgk', hs_re, c_re)
             - jnp.einsum('bcgp,gkp->bcgk', hs_im, c_im))
        return (hs_re[:, -1], hs_im[:, -1]), y

    init = (jnp.zeros((bsz, SSM_GROUPS, SSM_STATE), f32),
            jnp.zeros((bsz, SSM_GROUPS, SSM_STATE), f32))
    _, ys = lax.scan(step, init, u_chunks)
    y = ys.transpose(1, 0, 2, 3, 4).reshape(bsz, seq, dm) + d_skip.astype(f32) * uf
    z = jax.nn.gelu(y).astype(w_glu.dtype)
    val, gate = jnp.split(z @ w_glu, 2, axis=-1)
    return (val * jax.nn.sigmoid(gate)).astype(u.dtype)


def shortconv_mixer(u, w_in, w_conv, w_out):
    dm = u.shape[-1]
    b_gate, c_gate, v = jnp.split(u @ w_in, 3, axis=-1)
    cv = c_gate * v
    conv = lax.conv_general_dilated(
        cv, w_conv[:, None, :].astype(cv.dtype), window_strides=(1,),
        padding=[(CONV_WIDTH - 1, 0)],
        dimension_numbers=('NWC', 'WIO', 'NWC'), feature_group_count=dm)
    return (b_gate * conv) @ w_out


def swiglu_ffn(u, w_in, w_out):
    gate, up = jnp.split(u @ w_in, 2, axis=-1)
    return (jax.nn.silu(gate) * up) @ w_out


def setup_inputs(seed: int = 0) -> dict:
    key = jax.random.key(seed)
    ks = iter(jax.random.split(key, 32))
    f32 = jnp.float32

    def nrm(shape, scale):
        return jax.random.normal(next(ks), shape, f32) * scale

    na, nb = N_SSM_LAYERS, N_CONV_LAYERS
    g, pdim, k = SSM_GROUPS, SSM_STATE, SSM_GROUP
    x = nrm((BATCH, SEQ, D_MODEL), 1.0)
    p = nrm((DEPTH, BATCH, SEQ, PLE_DIM), 1.0)
    norm_mix_g = 1.0 + nrm((DEPTH, D_MODEL), 0.02)
    s5_a_re = -0.5 + nrm((na, g, pdim), 0.01)
    s5_a_im = math.pi * jnp.arange(pdim, dtype=f32) + nrm((na, g, pdim), 0.01)
    s5_log_dt = jax.random.uniform(next(ks), (na, g), f32, math.log(DT_MIN), math.log(DT_MAX))
    s5_b_re = nrm((na, g, pdim, k), (2 * k) ** -0.5)
    s5_b_im = nrm((na, g, pdim, k), (2 * k) ** -0.5)
    s5_c_re = nrm((na, g, k, pdim), pdim ** -0.5)
    s5_c_im = nrm((na, g, k, pdim), pdim ** -0.5)
    s5_d = nrm((na, D_MODEL), 1.0)
    s5_w_glu = nrm((na, D_MODEL, 2 * D_MODEL), D_MODEL ** -0.5)
    conv_w_in = nrm((nb, D_MODEL, 3 * D_MODEL), D_MODEL ** -0.5)
    conv_w = nrm((nb, CONV_WIDTH, D_MODEL), CONV_WIDTH ** -0.5)
    conv_w_out = nrm((nb, D_MODEL, D_MODEL), D_MODEL ** -0.5)
    norm_ffn_g = 1.0 + nrm((DEPTH, D_MODEL), 0.02)
    ffn_w_in = nrm((DEPTH, D_MODEL, 2 * FFN_HIDDEN), D_MODEL ** -0.5)
    ffn_w_out = nrm((DEPTH, FFN_HIDDEN, D_MODEL), FFN_HIDDEN ** -0.5)
    norm_ple_g = 1.0 + nrm((DEPTH, D_MODEL), 0.02)
    ple_w_gate = nrm((DEPTH, D_MODEL, D_MODEL), D_MODEL ** -0.5)
    ple_w_up = nrm((DEPTH, PLE_DIM, D_MODEL), PLE_DIM ** -0.5)
    final_norm_g = 1.0 + nrm((D_MODEL,), 0.02)
    return {
        "x": x, "p": p, "norm_mix_g": norm_mix_g,
        "s5_a_re": s5_a_re, "s5_a_im": s5_a_im, "s5_log_dt": s5_log_dt,
        "s5_b_re": s5_b_re, "s5_b_im": s5_b_im, "s5_c_re": s5_c_re, "s5_c_im": s5_c_im,
        "s5_d": s5_d, "s5_w_glu": s5_w_glu,
        "conv_w_in": conv_w_in, "conv_w": conv_w, "conv_w_out": conv_w_out,
        "norm_ffn_g": norm_ffn_g, "ffn_w_in": ffn_w_in, "ffn_w_out": ffn_w_out,
        "norm_ple_g": norm_ple_g, "ple_w_gate": ple_w_gate, "ple_w_up": ple_w_up,
        "final_norm_g": final_norm_g,
    }


def reference(x, p, norm_mix_g, s5_a_re, s5_a_im, s5_log_dt, s5_b_re, s5_b_im,
              s5_c_re, s5_c_im, s5_d, s5_w_glu, conv_w_in, conv_w, conv_w_out,
              norm_ffn_g, ffn_w_in, ffn_w_out, norm_ple_g, ple_w_gate, ple_w_up,
              final_norm_g):
    h = x
    for i in range(DEPTH):
        j = i // N_MIXERS
        hn = rmsnorm(h, norm_mix_g[i])
        if i % N_MIXERS == 0:
            mix = s5_mixer(hn, s5_a_re[j], s5_a_im[j], s5_log_dt[j], s5_b_re[j], s5_b_im[j],
                           s5_c_re[j], s5_c_im[j], s5_d[j], s5_w_glu[j])
        else:
            mix = shortconv_mixer(hn, conv_w_in[j], conv_w[j], conv_w_out[j])
        h = h + mix
        h = h + swiglu_ffn(rmsnorm(h, norm_ffn_g[i]), ffn_w_in[i], ffn_w_out[i])
        gate = jax.nn.sigmoid(rmsnorm(h, norm_ple_g[i]) @ ple_w_gate[i])
        h = h + gate * (p[i] @ ple_w_up[i])
    return rmsnorm(h, final_norm_g)
```

```python
import functools

import jax
import jax.numpy as jnp
from jax import lax
from jax.experimental import pallas as pl
from jax.experimental.pallas import tpu as pltpu

EPS = 1e-6
SSM_GROUP = 16
LANES = 128
SUBLANES = 8
MXU_DIM = 256
SLAB_GROUPS = MXU_DIM // SSM_GROUP
VMEM_LIMIT = 56 * 1024 * 1024

BF16 = jnp.bfloat16
F32 = jnp.float32


def _rms(x, g):
    return x * lax.rsqrt(jnp.mean(x * x, axis=-1, keepdims=True) + EPS) * g


def _dot(a, b):
    return jnp.dot(a, b, preferred_element_type=F32)


def _const_spec(shape):
    zeros = (0,) * len(shape)
    return pl.BlockSpec(shape, lambda i: zeros, pipeline_mode=pl.Buffered(1))


def _params(sem):
    return pltpu.CompilerParams(dimension_semantics=(sem,), vmem_limit_bytes=VMEM_LIMIT)


def _s5_prep_kernel(a_re_ref, a_im_ref, log_dt_ref, b_re_ref, b_im_ref,
                    abar_re_ref, abar_im_ref, bb_re_ref, bb_im_ref):
    a_re = a_re_ref[...]
    a_im = a_im_ref[...]
    dt = jnp.exp(log_dt_ref[...])
    mag = jnp.exp(dt * a_re)
    ang = dt * a_im
    abar_re = mag * jnp.cos(ang)
    abar_im = mag * jnp.sin(ang)
    nr = abar_re - 1.0
    ni = abar_im
    den = a_re * a_re + a_im * a_im
    f_re = ((nr * a_re + ni * a_im) / den)[:, None, :]
    f_im = ((ni * a_re - nr * a_im) / den)[:, None, :]
    b_re = b_re_ref[...]
    b_im = b_im_ref[...]
    abar_re_ref[...] = abar_re
    abar_im_ref[...] = abar_im
    bb_re_ref[...] = f_re * b_re - f_im * b_im
    bb_im_ref[...] = f_re * b_im + f_im * b_re


def _s5_prep(a_re, a_im, log_dt, b_re, b_im):
    g, p = a_re.shape
    k = b_re.shape[-1]
    gp = jax.ShapeDtypeStruct((g, p), F32)
    gkp = jax.ShapeDtypeStruct((g, k, p), F32)
    return pl.pallas_call(
        _s5_prep_kernel, out_shape=(gp, gp, gkp, gkp), name="s5_prep",
    )(a_re, a_im, log_dt[:, None], jnp.swapaxes(b_re, 1, 2), jnp.swapaxes(b_im, 1, 2))


def _s5_layout(abar_re, abar_im, bb_re, bb_im, c_re, c_im):
    g, k, p = bb_re.shape
    n_slab = g // SLAB_GROUPS
    pairs = SLAB_GROUPS // 2
    gl = jnp.arange(SLAB_GROUPS)[:, None, None]
    sel = (gl == 2 * jnp.arange(pairs)[None, :, None] + jnp.arange(2)[None, None, :]).astype(F32)
    bb = jnp.stack([bb_re, bb_im], axis=2).reshape(n_slab, SLAB_GROUPS, k, 2, p)
    b_mat = jnp.einsum("sgkrp,gqh->sgkqrhp", bb, sel).reshape(n_slab, SLAB_GROUPS * k, pairs * 4 * p)
    cc = jnp.stack([c_re, -c_im], axis=2).reshape(n_slab, SLAB_GROUPS, k, 2, p)
    c_mat = jnp.einsum("sgkrp,gqh->sqrhpgk", cc, sel).reshape(n_slab, pairs * 4 * p, SLAB_GROUPS * k)
    a_r = abar_re.reshape(g // 2, 2 * p)
    a_i = abar_im.reshape(g // 2, 2 * p)
    return b_mat.astype(BF16), c_mat.astype(BF16), a_r, a_i


def _s5_kernel(h_ref, g_ref, bm_ref, cm_ref, ar_ref, ai_ref, d_ref, wglu_ref,
               o_ref, bu_ref, y_ref, st_ref, *, n_steps, pair_chunk):
    n_slab = bm_ref.shape[0]
    slab_cols = bm_ref.shape[2]
    n_pairs = ar_ref.shape[0]
    d_model = h_ref.shape[1]

    @pl.when(pl.program_id(0) == 0)
    def _():
        st_ref[...] = jnp.zeros_like(st_ref)

    x = h_ref[...]
    u = _rms(x, g_ref[...])
    ub = u.astype(BF16)
    for s in range(n_slab):
        bu_ref[:, s * slab_cols:(s + 1) * slab_cols] = _dot(
            ub[:, s * MXU_DIM:(s + 1) * MXU_DIM], bm_ref[s])

    for c0 in range(0, n_pairs, pair_chunk):
        pairs = range(c0, c0 + pair_chunk)
        a_r = [jnp.broadcast_to(ar_ref[m:m + 1, :], (SUBLANES, LANES)) for m in pairs]
        a_i = [jnp.broadcast_to(ai_ref[m:m + 1, :], (SUBLANES, LANES)) for m in pairs]

        def step(t, carry, pairs=pairs, a_r=a_r, a_i=a_i):
            row = pl.multiple_of(t * SUBLANES, SUBLANES)
            out = []
            for j, m in enumerate(pairs):
                h_re, h_im = carry[j]
                re_cols = pl.ds(2 * m * LANES, LANES)
                im_cols = pl.ds((2 * m + 1) * LANES, LANES)
                n_re = a_r[j] * h_re - a_i[j] * h_im + bu_ref[pl.ds(row, SUBLANES), re_cols]
                n_im = a_r[j] * h_im + a_i[j] * h_re + bu_ref[pl.ds(row, SUBLANES), im_cols]
                bu_ref[pl.ds(row, SUBLANES), re_cols] = n_re
                bu_ref[pl.ds(row, SUBLANES), im_cols] = n_im
                out.append((n_re, n_im))
            return tuple(out)

        init = tuple((st_ref[:, 2 * m * LANES:(2 * m + 1) * LANES],
                      st_ref[:, (2 * m + 1) * LANES:(2 * m + 2) * LANES]) for m in pairs)
        last = lax.fori_loop(0, n_steps, step, init)
        for j, m in enumerate(pairs):
            st_ref[:, 2 * m * LANES:(2 * m + 1) * LANES] = last[j][0]
            st_ref[:, (2 * m + 1) * LANES:(2 * m + 2) * LANES] = last[j][1]

    for s in range(n_slab):
        hs = bu_ref[:, s * slab_cols:(s + 1) * slab_cols].astype(BF16)
        y_ref[:, s * MXU_DIM:(s + 1) * MXU_DIM] = _dot(hs, cm_ref[s])
    y = y_ref[...] + d_ref[...] * u
    z = jax.nn.gelu(y).astype(BF16)
    vg = _dot(z, wglu_ref[...])
    o_ref[...] = x + vg[:, :d_model] * jax.nn.sigmoid(vg[:, d_model:])


def _s5_layer(h, g, b_mat, c_mat, a_r, a_i, d_skip, w_glu, *, batch, steps):
    t_rows, d_model = h.shape
    rows = steps * batch
    n_state = b_mat.shape[0] * b_mat.shape[2]
    row_spec = pl.BlockSpec((rows, d_model), lambda i: (i, 0))
    kernel = functools.partial(_s5_kernel, n_steps=steps, pair_chunk=8)
    return pl.pallas_call(
        kernel,
        grid=(t_rows // rows,),
        in_specs=[row_spec, _const_spec((1, d_model)), _const_spec(b_mat.shape),
                  _const_spec(c_mat.shape), _const_spec(a_r.shape), _const_spec(a_i.shape),
                  _const_spec((1, d_model)), _const_spec(w_glu.shape)],
        out_specs=row_spec,
        out_shape=jax.ShapeDtypeStruct(h.shape, F32),
        scratch_shapes=[pltpu.VMEM((rows, n_state), F32),
                        pltpu.VMEM((rows, d_model), F32),
                        pltpu.VMEM((batch, n_state), F32)],
        compiler_params=_params("arbitrary"),
        name="s5_layer",
    )(h, g[None, :], b_mat, c_mat, a_r, a_i, d_skip[None, :], w_glu)


def _conv_kernel(h_ref, g_ref, win_ref, wc_ref, wout_ref, o_ref, ext_ref, *, batch):
    d_model = h_ref.shape[1]
    rows = h_ref.shape[0]
    halo = 2 * batch

    @pl.when(pl.program_id(0) == 0)
    def _():
        ext_ref[0:halo, :] = jnp.zeros((halo, d_model), F32)

    x = h_ref[...]
    xn = _rms(x, g_ref[...]).astype(BF16)
    c_gate = _dot(xn, win_ref[:, d_model:2 * d_model])
    v = _dot(xn, win_ref[:, 2 * d_model:])
    ext_ref[halo:, :] = c_gate * v
    conv = (wc_ref[0:1, :] * ext_ref[0:rows, :]
            + wc_ref[1:2, :] * ext_ref[batch:batch + rows, :]
            + wc_ref[2:3, :] * ext_ref[halo:, :])
    ext_ref[0:halo, :] = ext_ref[rows:, :]
    b_gate = _dot(xn, win_ref[:, :d_model])
    o_ref[...] = x + _dot((b_gate * conv).astype(BF16), wout_ref[...])


def _conv_layer(h, g, w_in, w_conv, w_out, *, batch, rows):
    t_rows, d_model = h.shape
    row_spec = pl.BlockSpec((rows, d_model), lambda i: (i, 0))
    return pl.pallas_call(
        functools.partial(_conv_kernel, batch=batch),
        grid=(t_rows // rows,),
        in_specs=[row_spec, _const_spec((1, d_model)), _const_spec(w_in.shape),
                  _const_spec(w_conv.shape), _const_spec(w_out.shape)],
        out_specs=row_spec,
        out_shape=jax.ShapeDtypeStruct(h.shape, F32),
        scratch_shapes=[pltpu.VMEM((rows + 2 * batch, d_model), F32)],
        compiler_params=_params("arbitrary"),
        name="conv_layer",
    )(h, g[None, :], w_in, w_conv, w_out)


def _ffn_kernel(h_ref, p_ref, gf_ref, win_ref, wout_ref, gp_ref, wgate_ref, wup_ref, gfin_ref,
                o_ref, hid_ref, *, final, hidden_chunks):
    hidden = wout_ref.shape[0]
    x = h_ref[...]
    xn = _rms(x, gf_ref[...]).astype(BF16)
    for lo, hi in hidden_chunks:
        gate = _dot(xn, win_ref[:, lo:hi])
        up = _dot(xn, win_ref[:, hidden + lo:hidden + hi])
        hid_ref[:, lo:hi] = (gate * jax.nn.sigmoid(gate) * up).astype(BF16)
    y = x + _dot(hid_ref[...], wout_ref[...])
    yn = _rms(y, gp_ref[...]).astype(BF16)
    ple_gate = jax.nn.sigmoid(_dot(yn, wgate_ref[...]))
    out = y + ple_gate * _dot(p_ref[...], wup_ref[...])
    if final:
        out = _rms(out, gfin_ref[...])
    o_ref[...] = out


def _chunks(total, size):
    return tuple((lo, min(lo + size, total)) for lo in range(0, total, size))


def _ffn_layer(h, p, g_ffn, w_in, w_out, g_ple, w_gate, w_up, g_final, *, rows, final):
    t_rows, d_model = h.shape
    hidden = w_out.shape[0]
    ple = p.shape[1]
    row_spec = pl.BlockSpec((rows, d_model), lambda i: (i, 0))
    kernel = functools.partial(_ffn_kernel, final=final, hidden_chunks=_chunks(hidden, 4 * MXU_DIM))
    return pl.pallas_call(
        kernel,
        grid=(t_rows // rows,),
        in_specs=[row_spec, pl.BlockSpec((rows, ple), lambda i: (i, 0)),
                  _const_spec((1, d_model)), _const_spec(w_in.shape), _const_spec(w_out.shape),
                  _const_spec((1, d_model)), _const_spec(w_gate.shape), _const_spec(w_up.shape),
                  _const_spec((1, d_model))],
        out_specs=row_spec,
        out_shape=jax.ShapeDtypeStruct(h.shape, F32),
        scratch_shapes=[pltpu.VMEM((rows, hidden), BF16)],
        compiler_params=_params("parallel"),
        name="ffn_layer",
    )(h, p, g_ffn[None, :], w_in, w_out, g_ple[None, :], w_gate, w_up, g_final[None, :])


def kernel(x, p, norm_mix_g, s5_a_re, s5_a_im, s5_log_dt, s5_b_re, s5_b_im, s5_c_re, s5_c_im, s5_d, s5_w_glu, conv_w_in, conv_w, conv_w_out, norm_ffn_g, ffn_w_in, ffn_w_out, norm_ple_g, ple_w_gate, ple_w_up, final_norm_g):
    batch, seq, d_model = x.shape
    depth = p.shape[0]
    assert batch == SUBLANES and d_model % MXU_DIM == 0
    rows = min(512, seq * batch)
    steps = min(32, seq)

    h = jnp.swapaxes(x, 0, 1).reshape(seq * batch, d_model)
    p_tm = jnp.swapaxes(p, 1, 2).astype(BF16).reshape(depth, seq * batch, p.shape[-1])
    for i in range(depth):
        j = i // 2
        if i % 2 == 0:
            prep = _s5_prep(s5_a_re[j], s5_a_im[j], s5_log_dt[j], s5_b_re[j], s5_b_im[j])
            b_mat, c_mat, a_r, a_i = _s5_layout(*prep, s5_c_re[j], s5_c_im[j])
            h = _s5_layer(h, norm_mix_g[i], b_mat, c_mat, a_r, a_i, s5_d[j],
                          s5_w_glu[j].astype(BF16), batch=batch, steps=steps)
        else:
            h = _conv_layer(h, norm_mix_g[i], conv_w_in[j].astype(BF16), conv_w[j],
                            conv_w_out[j].astype(BF16), batch=batch, rows=rows)
        h = _ffn_layer(h, p_tm[i], norm_ffn_g[i], ffn_w_in[i].astype(BF16),
                       ffn_w_out[i].astype(BF16), norm_ple_g[i], ple_w_gate[i].astype(BF16),
                       ple_w_up[i].astype(BF16), final_norm_g, rows=rows, final=(i == depth - 1))
    return jnp.swapaxes(h.reshape(seq, batch, d_model), 0, 1)
```

```python
import functools

import jax
import jax.numpy as jnp
from jax import lax
from jax.experimental import pallas as pl
from jax.experimental.pallas import tpu as pltpu

EPS = 1e-6
SSM_GROUP = 16
LANES = 128
SUBLANES = 8
MXU_DIM = 256
SLAB_GROUPS = MXU_DIM // SSM_GROUP
SLAB_PAIRS = SLAB_GROUPS // 2
VMEM_LIMIT = 56 * 1024 * 1024

BF16 = jnp.bfloat16
F32 = jnp.float32


def _rms(x, g):
    return x * lax.rsqrt(jnp.mean(x * x, axis=-1, keepdims=True) + EPS) * g


def _dot(a, b):
    return jnp.dot(a, b, preferred_element_type=F32)


def _const_spec(shape):
    zeros = (0,) * len(shape)
    return pl.BlockSpec(shape, lambda i: zeros, pipeline_mode=pl.Buffered(1))


def _layer_spec(stacked, layer):
    tail = (0,) * (stacked.ndim - 1)
    return pl.BlockSpec((None,) + stacked.shape[1:], lambda i: (layer,) + tail,
                        pipeline_mode=pl.Buffered(1))


def _params(sem):
    return pltpu.CompilerParams(dimension_semantics=(sem,), vmem_limit_bytes=VMEM_LIMIT)


def _to_time_major(blk):
    b, s, d = blk.shape
    return jnp.swapaxes(blk, 0, 1).reshape(s * b, d)


def _to_batch_major(rows, batch):
    n, d = rows.shape
    return jnp.swapaxes(rows.reshape(n // batch, batch, d), 0, 1)


def _s5_prep_kernel(a_re_ref, a_im_ref, log_dt_ref, b_re_ref, b_im_ref, ct_re_ref, ct_im_ref,
                    abar_re_ref, abar_im_ref, bm_ref, cm_ref):
    a_re = a_re_ref[...]
    a_im = a_im_ref[...]
    dt = jnp.exp(log_dt_ref[...])
    mag = jnp.exp(dt * a_re)
    ang = dt * a_im
    abar_re = mag * jnp.cos(ang)
    abar_im = mag * jnp.sin(ang)
    abar_re_ref[...] = abar_re
    abar_im_ref[...] = abar_im
    nr = abar_re - 1.0
    ni = abar_im
    den = a_re * a_re + a_im * a_im
    f_re = ((nr * a_re + ni * a_im) / den)[:, None, :]
    f_im = ((ni * a_re - nr * a_im) / den)[:, None, :]
    b_re = b_re_ref[...]
    b_im = b_im_ref[...]
    g, k, p = b_re.shape
    bb_re = (f_re * b_re - f_im * b_im).reshape(g * k, p)
    bb_im = (f_re * b_im + f_im * b_re).reshape(g * k, p)
    bb2 = (jnp.concatenate([bb_re, bb_re], axis=-1), jnp.concatenate([bb_im, bb_im], axis=-1))

    n_slab = bm_ref.shape[0]
    rows = SLAB_GROUPS * k
    row_group = lax.broadcasted_iota(jnp.int32, (rows, 2 * p), 0) // k
    lane_half = lax.broadcasted_iota(jnp.int32, (rows, 2 * p), 1) // p
    col_group = lax.broadcasted_iota(jnp.int32, (p, rows), 1) // k
    for s in range(n_slab):
        ct = (ct_re_ref[s], -ct_im_ref[s])
        for q in range(SLAB_PAIRS):
            own = row_group == 2 * q + lane_half
            for r in range(2):
                lo = (2 * q + r) * 2 * p
                blk = bb2[r][s * rows:(s + 1) * rows, :]
                bm_ref[s, :, lo:lo + 2 * p] = jnp.where(own, blk, 0.0).astype(BF16)
                for h in range(2):
                    cm_ref[s, lo + h * p:lo + (h + 1) * p, :] = jnp.where(
                        col_group == 2 * q + h, ct[r], 0.0).astype(BF16)


def _s5_prep(a_re, a_im, log_dt, b_re, b_im, c_re, c_im):
    g, p = a_re.shape
    k = b_re.shape[-1]
    n_slab = g // SLAB_GROUPS
    ct_re = c_re.reshape(n_slab, SLAB_GROUPS * k, p).swapaxes(1, 2)
    ct_im = c_im.reshape(n_slab, SLAB_GROUPS * k, p).swapaxes(1, 2)
    gp = jax.ShapeDtypeStruct((g, p), F32)
    abar_re, abar_im, b_mat, c_mat = pl.pallas_call(
        _s5_prep_kernel,
        out_shape=(gp, gp,
                   jax.ShapeDtypeStruct((n_slab, SLAB_GROUPS * k, SLAB_PAIRS * 4 * p), BF16),
                   jax.ShapeDtypeStruct((n_slab, SLAB_PAIRS * 4 * p, SLAB_GROUPS * k), BF16)),
        name="s5_prep",
    )(a_re, a_im, log_dt[:, None], jnp.swapaxes(b_re, 1, 2), jnp.swapaxes(b_im, 1, 2),
      ct_re, ct_im)
    return b_mat, c_mat, abar_re.reshape(g // 2, 2 * p), abar_im.reshape(g // 2, 2 * p)


def _s5_kernel(h_ref, g_ref, bm_ref, cm_ref, ar_ref, ai_ref, d_ref, wglu_ref,
               o_ref, bu_ref, y_ref, st_ref, *, n_steps, pair_chunk, batch_major_in):
    n_slab = bm_ref.shape[0]
    slab_cols = bm_ref.shape[2]
    n_pairs = ar_ref.shape[0]
    d_model = o_ref.shape[1]

    @pl.when(pl.program_id(0) == 0)
    def _():
        st_ref[...] = jnp.zeros_like(st_ref)

    x = _to_time_major(h_ref[...]) if batch_major_in else h_ref[...]
    u = _rms(x, g_ref[...])
    ub = u.astype(BF16)
    for s in range(n_slab):
        bu_ref[:, s * slab_cols:(s + 1) * slab_cols] = _dot(
            ub[:, s * MXU_DIM:(s + 1) * MXU_DIM], bm_ref[s])

    for c0 in range(0, n_pairs, pair_chunk):
        pairs = range(c0, c0 + pair_chunk)
        a_r = [jnp.broadcast_to(ar_ref[m:m + 1, :], (SUBLANES, LANES)) for m in pairs]
        a_i = [jnp.broadcast_to(ai_ref[m:m + 1, :], (SUBLANES, LANES)) for m in pairs]

        def step(t, carry, pairs=pairs, a_r=a_r, a_i=a_i):
            row = pl.multiple_of(t * SUBLANES, SUBLANES)
            out = []
            for j, m in enumerate(pairs):
                h_re, h_im = carry[j]
                re_cols = pl.ds(2 * m * LANES, LANES)
                im_cols = pl.ds((2 * m + 1) * LANES, LANES)
                n_re = a_r[j] * h_re - a_i[j] * h_im + bu_ref[pl.ds(row, SUBLANES), re_cols]
                n_im = a_r[j] * h_im + a_i[j] * h_re + bu_ref[pl.ds(row, SUBLANES), im_cols]
                bu_ref[pl.ds(row, SUBLANES), re_cols] = n_re
                bu_ref[pl.ds(row, SUBLANES), im_cols] = n_im
                out.append((n_re, n_im))
            return tuple(out)

        init = tuple((st_ref[:, 2 * m * LANES:(2 * m + 1) * LANES],
                      st_ref[:, (2 * m + 1) * LANES:(2 * m + 2) * LANES]) for m in pairs)
        last = lax.fori_loop(0, n_steps, step, init)
        for j, m in enumerate(pairs):
            st_ref[:, 2 * m * LANES:(2 * m + 1) * LANES] = last[j][0]
            st_ref[:, (2 * m + 1) * LANES:(2 * m + 2) * LANES] = last[j][1]

    for s in range(n_slab):
        hs = bu_ref[:, s * slab_cols:(s + 1) * slab_cols].astype(BF16)
        y_ref[:, s * MXU_DIM:(s + 1) * MXU_DIM] = _dot(hs, cm_ref[s])
    y = y_ref[...] + d_ref[...] * u
    z = jax.nn.gelu(y).astype(BF16)
    vg = _dot(z, wglu_ref[...])
    o_ref[...] = x + vg[:, :d_model] * jax.nn.sigmoid(vg[:, d_model:])


def _s5_layer(h, g, b_mat, c_mat, a_r, a_i, d_skip, w_glu, layer, *, batch, steps):
    batch_major_in = h.ndim == 3
    d_model = h.shape[-1]
    t_rows = h.size // d_model
    rows = steps * batch
    n_state = b_mat.shape[0] * b_mat.shape[2]
    row_spec = pl.BlockSpec((rows, d_model), lambda i: (i, 0))
    in_spec = pl.BlockSpec((batch, steps, d_model), lambda i: (0, i, 0)) if batch_major_in else row_spec
    kernel = functools.partial(_s5_kernel, n_steps=steps, pair_chunk=8,
                               batch_major_in=batch_major_in)
    return pl.pallas_call(
        kernel,
        grid=(t_rows // rows,),
        in_specs=[in_spec, _const_spec((1, d_model)), _const_spec(b_mat.shape),
                  _const_spec(c_mat.shape), _const_spec(a_r.shape), _const_spec(a_i.shape),
                  _const_spec((1, d_model)), _layer_spec(w_glu, layer)],
        out_specs=row_spec,
        out_shape=jax.ShapeDtypeStruct((t_rows, d_model), F32),
        scratch_shapes=[pltpu.VMEM((rows, n_state), F32),
                        pltpu.VMEM((rows, d_model), F32),
                        pltpu.VMEM((batch, n_state), F32)],
        compiler_params=_params("arbitrary"),
        name="s5_layer",
    )(h, g[None, :], b_mat, c_mat, a_r, a_i, d_skip[None, :], w_glu)


def _conv_kernel(h_ref, g_ref, win_ref, wc_ref, wout_ref, o_ref, ext_ref, *, batch):
    d_model = h_ref.shape[1]
    rows = h_ref.shape[0]
    halo = 2 * batch

    @pl.when(pl.program_id(0) == 0)
    def _():
        ext_ref[0:halo, :] = jnp.zeros((halo, d_model), F32)

    x = h_ref[...]
    xn = _rms(x, g_ref[...]).astype(BF16)
    c_gate = _dot(xn, win_ref[:, d_model:2 * d_model])
    v = _dot(xn, win_ref[:, 2 * d_model:])
    ext_ref[halo:, :] = c_gate * v
    conv = (wc_ref[0:1, :] * ext_ref[0:rows, :]
            + wc_ref[1:2, :] * ext_ref[batch:batch + rows, :]
            + wc_ref[2:3, :] * ext_ref[halo:, :])
    ext_ref[0:halo, :] = ext_ref[rows:, :]
    b_gate = _dot(xn, win_ref[:, :d_model])
    o_ref[...] = x + _dot((b_gate * conv).astype(BF16), wout_ref[...])


def _conv_layer(h, g, w_in, w_conv, w_out, layer, *, batch, rows):
    t_rows, d_model = h.shape
    row_spec = pl.BlockSpec((rows, d_model), lambda i: (i, 0))
    return pl.pallas_call(
        functools.partial(_conv_kernel, batch=batch),
        grid=(t_rows // rows,),
        in_specs=[row_spec, _const_spec((1, d_model)), _layer_spec(w_in, layer),
                  _layer_spec(w_conv, layer), _layer_spec(w_out, layer)],
        out_specs=row_spec,
        out_shape=jax.ShapeDtypeStruct(h.shape, F32),
        scratch_shapes=[pltpu.VMEM((rows + 2 * batch, d_model), F32)],
        compiler_params=_params("arbitrary"),
        name="conv_layer",
    )(h, g[None, :], w_in, w_conv, w_out)


def _ffn_kernel(h_ref, p_ref, gf_ref, win_ref, wout_ref, gp_ref, wgate_ref, wup_ref, gfin_ref,
                o_ref, hid_ref, *, final, hidden_chunks):
    hidden = wout_ref.shape[0]
    x = h_ref[...]
    xn = _rms(x, gf_ref[...]).astype(BF16)
    for lo, hi in hidden_chunks:
        gate = _dot(xn, win_ref[:, lo:hi])
        up = _dot(xn, win_ref[:, hidden + lo:hidden + hi])
        hid_ref[:, lo:hi] = (gate * jax.nn.sigmoid(gate) * up).astype(BF16)
    y = x + _dot(hid_ref[...], wout_ref[...])
    yn = _rms(y, gp_ref[...]).astype(BF16)
    ple_gate = jax.nn.sigmoid(_dot(yn, wgate_ref[...]))
    p_rows = _to_time_major(p_ref[...]).astype(BF16)
    out = y + ple_gate * _dot(p_rows, wup_ref[...])
    if final:
        o_ref[...] = _to_batch_major(_rms(out, gfin_ref[...]), p_ref.shape[0])
    else:
        o_ref[...] = out


def _chunks(total, size):
    return tuple((lo, min(lo + size, total)) for lo in range(0, total, size))


def _ffn_layer(h, p, g_ffn, w_in, w_out, g_ple, w_gate, w_up, g_final, layer, *, steps, final):
    t_rows, d_model = h.shape
    _, batch, seq, ple = p.shape
    hidden = w_out.shape[1]
    rows = steps * batch
    row_spec = pl.BlockSpec((rows, d_model), lambda i: (i, 0))
    if final:
        out_spec = pl.BlockSpec((batch, steps, d_model), lambda i: (0, i, 0))
        out_shape = jax.ShapeDtypeStruct((batch, seq, d_model), F32)
    else:
        out_spec, out_shape = row_spec, jax.ShapeDtypeStruct(h.shape, F32)
    kernel = functools.partial(_ffn_kernel, final=final, hidden_chunks=_chunks(hidden, 4 * MXU_DIM))
    return pl.pallas_call(
        kernel,
        grid=(t_rows // rows,),
        in_specs=[row_spec, pl.BlockSpec((None, batch, steps, ple), lambda i: (layer, 0, i, 0)),
                  _const_spec((1, d_model)), _layer_spec(w_in, layer), _layer_spec(w_out, layer),
                  _const_spec((1, d_model)), _layer_spec(w_gate, layer), _layer_spec(w_up, layer),
                  _const_spec((1, d_model))],
        out_specs=out_spec,
        out_shape=out_shape,
        scratch_shapes=[pltpu.VMEM((rows, hidden), BF16)],
        compiler_params=_params("parallel"),
        name="ffn_layer",
    )(h, p, g_ffn[None, :], w_in, w_out, g_ple[None, :], w_gate, w_up, g_final[None, :])


def kernel(x, p, norm_mix_g, s5_a_re, s5_a_im, s5_log_dt, s5_b_re, s5_b_im, s5_c_re, s5_c_im, s5_d, s5_w_glu, conv_w_in, conv_w, conv_w_out, norm_ffn_g, ffn_w_in, ffn_w_out, norm_ple_g, ple_w_gate, ple_w_up, final_norm_g):
    batch, seq, d_model = x.shape
    depth = p.shape[0]
    assert batch == SUBLANES and d_model % MXU_DIM == 0
    steps = min(64, seq)

    s5_w_glu, conv_w_in, conv_w_out, ffn_w_in, ffn_w_out, ple_w_gate, ple_w_up = (
        w.astype(BF16) for w in (s5_w_glu, conv_w_in, conv_w_out, ffn_w_in, ffn_w_out,
                                 ple_w_gate, ple_w_up))
    h = x
    for i in range(depth):
        j = i // 2
        if i % 2 == 0:
            b_mat, c_mat, a_r, a_i = _s5_prep(s5_a_re[j], s5_a_im[j], s5_log_dt[j], s5_b_re[j],
                                              s5_b_im[j], s5_c_re[j], s5_c_im[j])
            h = _s5_layer(h, norm_mix_g[i], b_mat, c_mat, a_r, a_i, s5_d[j], s5_w_glu, j,
                          batch=batch, steps=steps)
        else:
            h = _conv_layer(h, norm_mix_g[i], conv_w_in, conv_w, conv_w_out, j,
                            batch=batch, rows=steps * batch)
        h = _ffn_layer(h, p, norm_ffn_g[i], ffn_w_in, ffn_w_out, norm_ple_g[i], ple_w_gate,
                       ple_w_up, final_norm_g, i, steps=steps, final=(i == depth - 1))
    return h
```

```python
import functools

import jax
import jax.numpy as jnp
from jax import lax
from jax.experimental import pallas as pl
from jax.experimental.pallas import tpu as pltpu

EPS = 1e-6
SSM_GROUP = 16
LANES = 128
SUBLANES = 8
MXU_DIM = 256
CHUNK = MXU_DIM // SSM_GROUP
VMEM_LIMIT = 56 * 1024 * 1024

BF16 = jnp.bfloat16
F32 = jnp.float32


def _rms(x, g):
    return x * lax.rsqrt(jnp.mean(x * x, axis=-1, keepdims=True) + EPS) * g


def _dot(a, b):
    return jnp.dot(a, b, preferred_element_type=F32)


def _const_spec(shape):
    zeros = (0,) * len(shape)
    return pl.BlockSpec(shape, lambda i: zeros, pipeline_mode=pl.Buffered(1))


def _layer_spec(stacked, layer):
    tail = (0,) * (stacked.ndim - 1)
    return pl.BlockSpec((None,) + stacked.shape[1:], lambda i: (layer,) + tail,
                        pipeline_mode=pl.Buffered(1))


def _params(sem):
    return pltpu.CompilerParams(dimension_semantics=(sem,), vmem_limit_bytes=VMEM_LIMIT)


def _to_time_major(blk):
    b, s, d = blk.shape
    return jnp.swapaxes(blk, 0, 1).reshape(s * b, d)


def _to_batch_major(rows, batch):
    n, d = rows.shape
    return jnp.swapaxes(rows.reshape(n // batch, batch, d), 0, 1)


def _s5_prep_kernel(a_re_ref, a_im_ref, log_dt_ref, b_re_ref, b_im_ref, c_re_ref, c_im_ref,
                    w_ref, v_ref, z_ref, coef_ref):
    n_groups, k, p = b_re_ref.shape
    width = CHUNK * k
    first_half = lax.broadcasted_iota(jnp.int32, (1, 2 * p), 1) < p
    neg_first = jnp.where(first_half, -1.0, 1.0)
    lane_block = lax.broadcasted_iota(jnp.int32, (width, width), 1) // k

    def swap(v):
        return pltpu.roll(v, p, axis=1)

    def twice(v):
        return jnp.concatenate([v, v], axis=-1)

    for i in range(n_groups):
        a_re = a_re_ref[i]
        a_im = a_im_ref[i]
        dt = jnp.exp(log_dt_ref[i])
        mag = jnp.exp(dt * a_re)
        ang = dt * a_im
        abar_re = mag * jnp.cos(ang)
        abar_im = mag * jnp.sin(ang)
        nr = abar_re - 1.0
        ni = abar_im
        den = a_re * a_re + a_im * a_im
        f_re = (nr * a_re + ni * a_im) / den
        f_im = (ni * a_re - nr * a_im) / den
        b_re = b_re_ref[i]
        b_im = b_im_ref[i]
        bb_re = twice(f_re * b_re - f_im * b_im)
        bb_im = twice(f_re * b_im + f_im * b_re)
        c_re = c_re_ref[i]
        c_im = c_im_ref[i]

        mul_re = twice(abar_re)
        mul_im = jnp.concatenate([-abar_im, abar_im], axis=-1)
        power = jnp.where(first_half, 1.0, 0.0)
        powers = [power]
        for _ in range(CHUNK):
            power = power * mul_re + swap(power) * mul_im
            powers.append(power)

        bp = jnp.concatenate(
            [bb_re * powers[CHUNK - 1 - rho] + bb_im * (swap(powers[CHUNK - 1 - rho]) * neg_first)
             for rho in range(CHUNK)], axis=0)
        cc = jnp.concatenate([jnp.concatenate([c_re, -c_im], axis=-1)] * CHUNK, axis=0)
        gx = lax.dot_general(bp, cc, (((1,), (1,)), ((), ())), precision=lax.Precision.HIGHEST,
                             preferred_element_type=F32)
        w = jnp.zeros((width, width), F32)
        for t in range(CHUNK):
            up = k * (CHUNK - 1 - t)
            shifted = gx if up == 0 else jnp.concatenate(
                [gx[up:, :], jnp.zeros((up, width), F32)], axis=0)
            w = jnp.where(lane_block == t, shifted, w)
        w_ref[i] = w.astype(BF16)
        v_ref[i] = jnp.concatenate([bp, swap(bp)], axis=-1).astype(BF16)

        cr2 = twice(c_re)
        ci2 = twice(c_im)
        zt = jnp.concatenate(
            [cr2 * (powers[t + 1] * -neg_first) - ci2 * swap(powers[t + 1])
             for t in range(CHUNK)], axis=0)
        z_ref[i] = zt.T.astype(BF16)

        last = powers[CHUNK]
        re2 = jnp.where(first_half, last, swap(last))
        im2 = jnp.where(first_half, swap(last), last) * neg_first
        coef_ref[i] = jnp.concatenate(
            [re2, im2, -im2, jnp.zeros((SUBLANES - 3, 2 * p), F32)], axis=0)


def _s5_prep(a_re, a_im, log_dt, b_re, b_im, c_re, c_im):
    g, p = a_re.shape
    k = b_re.shape[-1]
    assert CHUNK * k == MXU_DIM and 2 * p == LANES
    per_step = SUBLANES
    width = CHUNK * k

    def spec(*tail):
        return pl.BlockSpec((per_step,) + tail, lambda i: (i,) + (0,) * len(tail))

    return pl.pallas_call(
        _s5_prep_kernel,
        grid=(g // per_step,),
        in_specs=[spec(1, p), spec(1, p), spec(1, 1), spec(k, p), spec(k, p), spec(k, p), spec(k, p)],
        out_specs=(spec(width, width), spec(width, 2 * LANES), spec(LANES, width), spec(SUBLANES, LANES)),
        out_shape=(jax.ShapeDtypeStruct((g, width, width), BF16),
                   jax.ShapeDtypeStruct((g, width, 2 * LANES), BF16),
                   jax.ShapeDtypeStruct((g, LANES, width), BF16),
                   jax.ShapeDtypeStruct((g, SUBLANES, LANES), F32)),
        compiler_params=_params("parallel"),
        name="s5_prep",
    )(a_re[:, None, :], a_im[:, None, :], log_dt[:, None, None], jnp.swapaxes(b_re, 1, 2),
      jnp.swapaxes(b_im, 1, 2), c_re, c_im)


def _block_transpose8(tiles, lane_block):
    v = list(tiles)
    for d in (4, 2, 1):
        high = (lane_block & d) != 0
        for i in range(8):
            if i & d:
                continue
            lo, hi = v[i], v[i + d]
            v[i] = jnp.where(high, pltpu.roll(hi, SSM_GROUP * d, axis=1), lo)
            v[i + d] = jnp.where(high, hi, pltpu.roll(lo, LANES - SSM_GROUP * d, axis=1))
    return v


def _s5_kernel(h_ref, g_ref, w_ref, v_ref, z_ref, coef_ref, d_ref, wglu_ref,
               o_ref, u_ref, xs_ref, st_ref, *, batch_major_in, glu_rows):
    rows, d_model = o_ref.shape
    n_chunks = rows // (CHUNK * SUBLANES)
    tile_groups = LANES // SSM_GROUP

    @pl.when(pl.program_id(0) == 0)
    def _():
        st_ref[...] = jnp.zeros_like(st_ref)

    if batch_major_in:
        o_ref[...] = _to_time_major(h_ref[...])
        x_ref = o_ref
    else:
        x_ref = h_ref
    u_ref[...] = _rms(x_ref[...], g_ref[...])
    lane_block = lax.broadcasted_iota(jnp.int32, (SUBLANES, LANES), 1) // SSM_GROUP

    def step_rows(c, half, tau):
        lo = ((c * CHUNK + half * SUBLANES) + tau) * SUBLANES
        return slice(lo, lo + SUBLANES)

    for j in range(d_model // LANES):
        lanes = slice(j * LANES, (j + 1) * LANES)
        for cp in range(n_chunks // 2):
            for half in range(2):
                pair = [_block_transpose8([u_ref[step_rows(c, half, tau), lanes] for tau in range(8)],
                                          lane_block) for c in (2 * cp, 2 * cp + 1)]
                for gg in range(tile_groups):
                    xs_ref[j * tile_groups + gg, cp * 16:(cp + 1) * 16, half * LANES:(half + 1) * LANES] = (
                        jnp.concatenate([pair[0][gg], pair[1][gg]], axis=0).astype(BF16))
        ys = []
        for gg in range(tile_groups):
            g = j * tile_groups + gg
            xg = xs_ref[g]
            s12 = _dot(xg, v_ref[g])
            a_re = jnp.broadcast_to(coef_ref[g, 0:1, :], (SUBLANES, LANES))
            a_im = jnp.broadcast_to(coef_ref[g, 1:2, :], (SUBLANES, LANES))
            a_im_neg = jnp.broadcast_to(coef_ref[g, 2:3, :], (SUBLANES, LANES))
            x1 = st_ref[g, :, :LANES]
            x2 = st_ref[g, :, LANES:]
            starts = []
            for c in range(n_chunks):
                starts.append(x1)
                r = slice(c * SUBLANES, (c + 1) * SUBLANES)
                x1, x2 = (a_re * x1 + a_im * x2 + s12[r, :LANES],
                          a_re * x2 + a_im_neg * x1 + s12[r, LANES:])
            st_ref[g, :, :LANES] = x1
            st_ref[g, :, LANES:] = x2
            hs = jnp.concatenate(starts, axis=0).astype(BF16)
            ys.append(_dot(xg, w_ref[g]) + _dot(hs, z_ref[g]))
        d_tile = d_ref[:, lanes]
        for c in range(n_chunks):
            for half in range(2):
                r = slice(c * SUBLANES, (c + 1) * SUBLANES)
                back = _block_transpose8(
                    [ys[gg][r, half * LANES:(half + 1) * LANES] for gg in range(tile_groups)],
                    lane_block)
                for tau in range(8):
                    t_rows = step_rows(c, half, tau)
                    u_ref[t_rows, lanes] = d_tile * u_ref[t_rows, lanes] + back[tau]

    for lo in range(0, rows, glu_rows):
        r = slice(lo, lo + glu_rows)
        z = jax.nn.gelu(u_ref[r, :]).astype(BF16)
        vg = _dot(z, wglu_ref[...])
        o_ref[r, :] = x_ref[r, :] + vg[:, :d_model] * jax.nn.sigmoid(vg[:, d_model:])


def _s5_layer(h, g, w_mat, v_mat, z_mat, coef, d_skip, w_glu, layer, *, batch, steps):
    batch_major_in = h.ndim == 3
    d_model = h.shape[-1]
    t_rows = h.size // d_model
    rows = steps * batch
    assert steps % (2 * CHUNK) == 0
    row_spec = pl.BlockSpec((rows, d_model), lambda i: (i, 0))
    in_spec = pl.BlockSpec((batch, steps, d_model), lambda i: (0, i, 0)) if batch_major_in else row_spec
    kernel = functools.partial(_s5_kernel, batch_major_in=batch_major_in, glu_rows=min(256, rows))
    return pl.pallas_call(
        kernel,
        grid=(t_rows // rows,),
        in_specs=[in_spec, _const_spec((1, d_model)), _const_spec(w_mat.shape),
                  _const_spec(v_mat.shape), _const_spec(z_mat.shape), _const_spec(coef.shape),
                  _const_spec((1, d_model)), _layer_spec(w_glu, layer)],
        out_specs=row_spec,
        out_shape=jax.ShapeDtypeStruct((t_rows, d_model), F32),
        scratch_shapes=[pltpu.VMEM((rows, d_model), F32),
                        pltpu.VMEM((d_model // SSM_GROUP, rows // CHUNK, MXU_DIM), BF16),
                        pltpu.VMEM((coef.shape[0], batch, 2 * LANES), F32)],
        compiler_params=_params("arbitrary"),
        name="s5_layer",
    )(h, g[None, :], w_mat, v_mat, z_mat, coef, d_skip[None, :], w_glu)


def _conv_kernel(h_ref, g_ref, win_ref, wc_ref, wout_ref, o_ref, ext_ref, *, batch):
    d_model = h_ref.shape[1]
    rows = h_ref.shape[0]
    halo = 2 * batch

    @pl.when(pl.program_id(0) == 0)
    def _():
        ext_ref[0:halo, :] = jnp.zeros((halo, d_model), F32)

    x = h_ref[...]
    xn = _rms(x, g_ref[...]).astype(BF16)
    c_gate = _dot(xn, win_ref[:, d_model:2 * d_model])
    v = _dot(xn, win_ref[:, 2 * d_model:])
    ext_ref[halo:, :] = c_gate * v
    conv = (wc_ref[0:1, :] * ext_ref[0:rows, :]
            + wc_ref[1:2, :] * ext_ref[batch:batch + rows, :]
            + wc_ref[2:3, :] * ext_ref[halo:, :])
    ext_ref[0:halo, :] = ext_ref[rows:, :]
    b_gate = _dot(xn, win_ref[:, :d_model])
    o_ref[...] = x + _dot((b_gate * conv).astype(BF16), wout_ref[...])


def _conv_layer(h, g, w_in, w_conv, w_out, layer, *, batch, rows):
    t_rows, d_model = h.shape
    row_spec = pl.BlockSpec((rows, d_model), lambda i: (i, 0))
    return pl.pallas_call(
        functools.partial(_conv_kernel, batch=batch),
        grid=(t_rows // rows,),
        in_specs=[row_spec, _const_spec((1, d_model)), _layer_spec(w_in, layer),
                  _layer_spec(w_conv, layer), _layer_spec(w_out, layer)],
        out_specs=row_spec,
        out_shape=jax.ShapeDtypeStruct(h.shape, F32),
        scratch_shapes=[pltpu.VMEM((rows + 2 * batch, d_model), F32)],
        compiler_params=_params("arbitrary"),
        name="conv_layer",
    )(h, g[None, :], w_in, w_conv, w_out)


def _ffn_kernel(h_ref, p_ref, gf_ref, win_ref, wout_ref, gp_ref, wgate_ref, wup_ref, gfin_ref,
                o_ref, hid_ref, *, final, hidden_chunks):
    hidden = wout_ref.shape[0]
    x = h_ref[...]
    xn = _rms(x, gf_ref[...]).astype(BF16)
    for lo, hi in hidden_chunks:
        gate = _dot(xn, win_ref[:, lo:hi])
        up = _dot(xn, win_ref[:, hidden + lo:hidden + hi])
        hid_ref[:, lo:hi] = (gate * jax.nn.sigmoid(gate) * up).astype(BF16)
    y = x + _dot(hid_ref[...], wout_ref[...])
    yn = _rms(y, gp_ref[...]).astype(BF16)
    ple_gate = jax.nn.sigmoid(_dot(yn, wgate_ref[...]))
    p_rows = _to_time_major(p_ref[...]).astype(BF16)
    out = y + ple_gate * _dot(p_rows, wup_ref[...])
    if final:
        o_ref[...] = _to_batch_major(_rms(out, gfin_ref[...]), p_ref.shape[0])
    else:
        o_ref[...] = out


def _chunks(total, size):
    return tuple((lo, min(lo + size, total)) for lo in range(0, total, size))


def _ffn_layer(h, p, g_ffn, w_in, w_out, g_ple, w_gate, w_up, g_final, layer, *, steps, final):
    t_rows, d_model = h.shape
    _, batch, seq, ple = p.shape
    hidden = w_out.shape[1]
    rows = steps * batch
    row_spec = pl.BlockSpec((rows, d_model), lambda i: (i, 0))
    if final:
        out_spec = pl.BlockSpec((batch, steps, d_model), lambda i: (0, i, 0))
        out_shape = jax.ShapeDtypeStruct((batch, seq, d_model), F32)
    else:
        out_spec, out_shape = row_spec, jax.ShapeDtypeStruct(h.shape, F32)
    kernel = functools.partial(_ffn_kernel, final=final, hidden_chunks=_chunks(hidden, 4 * MXU_DIM))
    return pl.pallas_call(
        kernel,
        grid=(t_rows // rows,),
        in_specs=[row_spec, pl.BlockSpec((None, batch, steps, ple), lambda i: (layer, 0, i, 0)),
                  _const_spec((1, d_model)), _layer_spec(w_in, layer), _layer_spec(w_out, layer),
                  _const_spec((1, d_model)), _layer_spec(w_gate, layer), _layer_spec(w_up, layer),
                  _const_spec((1, d_model))],
        out_specs=out_spec,
        out_shape=out_shape,
        scratch_shapes=[pltpu.VMEM((rows, hidden), BF16)],
        compiler_params=_params("parallel"),
        name="ffn_layer",
    )(h, p, g_ffn[None, :], w_in, w_out, g_ple[None, :], w_gate, w_up, g_final[None, :])


def kernel(x, p, norm_mix_g, s5_a_re, s5_a_im, s5_log_dt, s5_b_re, s5_b_im, s5_c_re, s5_c_im, s5_d, s5_w_glu, conv_w_in, conv_w, conv_w_out, norm_ffn_g, ffn_w_in, ffn_w_out, norm_ple_g, ple_w_gate, ple_w_up, final_norm_g):
    batch, seq, d_model = x.shape
    depth = p.shape[0]
    assert batch == SUBLANES and d_model % MXU_DIM == 0
    steps = min(64, seq)

    s5_w_glu, conv_w_in, conv_w_out, ffn_w_in, ffn_w_out, ple_w_gate, ple_w_up = (
        w.astype(BF16) for w in (s5_w_glu, conv_w_in, conv_w_out, ffn_w_in, ffn_w_out,
                                 ple_w_gate, ple_w_up))
    h = x
    for i in range(depth):
        j = i // 2
        if i % 2 == 0:
            mats = _s5_prep(s5_a_re[j], s5_a_im[j], s5_log_dt[j], s5_b_re[j], s5_b_im[j],
                            s5_c_re[j], s5_c_im[j])
            h = _s5_layer(h, norm_mix_g[i], *mats, s5_d[j], s5_w_glu, j,
                          batch=batch, steps=min(2 * steps, seq))
        else:
            h = _conv_layer(h, norm_mix_g[i], conv_w_in, conv_w, conv_w_out, j,
                            batch=batch, rows=steps * batch)
        h = _ffn_layer(h, p, norm_ffn_g[i], ffn_w_in, ffn_w_out, norm_ple_g[i], ple_w_gate,
                       ple_w_up, final_norm_g, i, steps=steps, final=(i == depth - 1))
    return h
```

```python
import functools

import jax
import jax.numpy as jnp
from jax import lax
from jax.experimental import pallas as pl
from jax.experimental.pallas import tpu as pltpu

EPS = 1e-6
SSM_GROUP = 16
LANES = 128
SUBLANES = 8
MXU_DIM = 256
CHUNK = MXU_DIM // SSM_GROUP
VMEM_LIMIT = 56 * 1024 * 1024

BF16 = jnp.bfloat16
F32 = jnp.float32


def _rms(x, g):
    return x * lax.rsqrt(jnp.mean(x * x, axis=-1, keepdims=True) + EPS) * g


def _dot(a, b):
    return jnp.dot(a, b, preferred_element_type=F32)


def _const_spec(shape):
    zeros = (0,) * len(shape)
    return pl.BlockSpec(shape, lambda i: zeros, pipeline_mode=pl.Buffered(1))


def _layer_spec(stacked, layer):
    tail = (0,) * (stacked.ndim - 1)
    return pl.BlockSpec((None,) + stacked.shape[1:], lambda i: (layer,) + tail,
                        pipeline_mode=pl.Buffered(1))


def _params(sem):
    return pltpu.CompilerParams(dimension_semantics=(sem,), vmem_limit_bytes=VMEM_LIMIT)


def _to_time_major(blk):
    b, s, d = blk.shape
    return jnp.swapaxes(blk, 0, 1).reshape(s * b, d)


def _to_batch_major(rows, batch):
    n, d = rows.shape
    return jnp.swapaxes(rows.reshape(n // batch, batch, d), 0, 1)


def _s5_prep_kernel(a_re_ref, a_im_ref, log_dt_ref, b_re_ref, b_im_ref, c_re_ref, c_im_ref,
                    w_ref, v_ref, z_ref, coef_ref):
    n_groups, k, p = b_re_ref.shape
    width = CHUNK * k
    first_half = lax.broadcasted_iota(jnp.int32, (1, 2 * p), 1) < p
    neg_first = jnp.where(first_half, -1.0, 1.0)
    lane_block = lax.broadcasted_iota(jnp.int32, (width, width), 1) // k

    def swap(v):
        return pltpu.roll(v, p, axis=1)

    def twice(v):
        return jnp.concatenate([v, v], axis=-1)

    for i in range(n_groups):
        a_re = a_re_ref[i]
        a_im = a_im_ref[i]
        dt = jnp.exp(log_dt_ref[i])
        mag = jnp.exp(dt * a_re)
        ang = dt * a_im
        abar_re = mag * jnp.cos(ang)
        abar_im = mag * jnp.sin(ang)
        nr = abar_re - 1.0
        ni = abar_im
        den = a_re * a_re + a_im * a_im
        f_re = (nr * a_re + ni * a_im) / den
        f_im = (ni * a_re - nr * a_im) / den
        b_re = b_re_ref[i]
        b_im = b_im_ref[i]
        bb_re = twice(f_re * b_re - f_im * b_im)
        bb_im = twice(f_re * b_im + f_im * b_re)
        c_re = c_re_ref[i]
        c_im = c_im_ref[i]

        mul_re = twice(abar_re)
        mul_im = jnp.concatenate([-abar_im, abar_im], axis=-1)
        power = jnp.where(first_half, 1.0, 0.0)
        powers = [power]
        for _ in range(CHUNK):
            power = power * mul_re + swap(power) * mul_im
            powers.append(power)

        bp = jnp.concatenate(
            [bb_re * powers[CHUNK - 1 - rho] + bb_im * (swap(powers[CHUNK - 1 - rho]) * neg_first)
             for rho in range(CHUNK)], axis=0)
        cc = jnp.concatenate([jnp.concatenate([c_re, -c_im], axis=-1)] * CHUNK, axis=0)
        gx = lax.dot_general(bp, cc, (((1,), (1,)), ((), ())), precision=lax.Precision.HIGHEST,
                             preferred_element_type=F32)
        w = jnp.zeros((width, width), F32)
        for t in range(CHUNK):
            up = k * (CHUNK - 1 - t)
            shifted = gx if up == 0 else jnp.concatenate(
                [gx[up:, :], jnp.zeros((up, width), F32)], axis=0)
            w = jnp.where(lane_block == t, shifted, w)
        w_ref[i] = w.astype(BF16)
        v_ref[i] = jnp.concatenate([bp, swap(bp)], axis=-1).astype(BF16)

        cr2 = twice(c_re)
        ci2 = twice(c_im)
        zt = jnp.concatenate(
            [cr2 * (powers[t + 1] * -neg_first) - ci2 * swap(powers[t + 1])
             for t in range(CHUNK)], axis=0)
        z_ref[i] = zt.T.astype(BF16)

        last = powers[CHUNK]
        re2 = jnp.where(first_half, last, swap(last))
        im2 = jnp.where(first_half, swap(last), last) * neg_first
        coef_ref[i] = jnp.concatenate(
            [re2, im2, -im2, jnp.zeros((SUBLANES - 3, 2 * p), F32)], axis=0)


def _s5_prep(a_re, a_im, log_dt, b_re, b_im, c_re, c_im):
    g, p = a_re.shape
    k = b_re.shape[-1]
    assert CHUNK * k == MXU_DIM and 2 * p == LANES
    per_step = SUBLANES
    width = CHUNK * k

    def spec(*tail):
        return pl.BlockSpec((per_step,) + tail, lambda i: (i,) + (0,) * len(tail))

    return pl.pallas_call(
        _s5_prep_kernel,
        grid=(g // per_step,),
        in_specs=[spec(1, p), spec(1, p), spec(1, 1), spec(k, p), spec(k, p), spec(k, p), spec(k, p)],
        out_specs=(spec(width, width), spec(width, 2 * LANES), spec(LANES, width), spec(SUBLANES, LANES)),
        out_shape=(jax.ShapeDtypeStruct((g, width, width), BF16),
                   jax.ShapeDtypeStruct((g, width, 2 * LANES), BF16),
                   jax.ShapeDtypeStruct((g, LANES, width), BF16),
                   jax.ShapeDtypeStruct((g, SUBLANES, LANES), F32)),
        compiler_params=_params("parallel"),
        name="s5_prep",
    )(a_re[:, None, :], a_im[:, None, :], log_dt[:, None, None], jnp.swapaxes(b_re, 1, 2),
      jnp.swapaxes(b_im, 1, 2), c_re, c_im)


def _block_transpose8(tiles, lane_block):
    v = list(tiles)
    for d in (4, 2, 1):
        high = (lane_block & d) != 0
        for i in range(8):
            if i & d:
                continue
            lo, hi = v[i], v[i + d]
            v[i] = jnp.where(high, pltpu.roll(hi, SSM_GROUP * d, axis=1), lo)
            v[i + d] = jnp.where(high, hi, pltpu.roll(lo, LANES - SSM_GROUP * d, axis=1))
    return v


def _s5_kernel(h_ref, g_ref, w_ref, v_ref, z_ref, coef_ref, d_ref, wglu_ref,
               o_ref, u_ref, xs_ref, st_ref, act_ref, *, batch_major_in):
    rows, d_model = o_ref.shape
    n_chunks = rows // (CHUNK * SUBLANES)
    tile_groups = LANES // SSM_GROUP
    slot = pl.program_id(0) % 2

    @pl.when(pl.program_id(0) == 0)
    def _():
        st_ref[...] = jnp.zeros_like(st_ref)
        act_ref[...] = jnp.zeros_like(act_ref)

    prev = act_ref[1 - slot]

    def glu_columns(lo):
        val = _dot(prev, wglu_ref[:, lo:lo + MXU_DIM])
        gate = _dot(prev, wglu_ref[:, d_model + lo:d_model + lo + MXU_DIM])
        o_ref[:, lo:lo + MXU_DIM] = val * jax.nn.sigmoid(gate)

    x = _to_time_major(h_ref[...]) if batch_major_in else h_ref[...]
    u_ref[...] = _rms(x, g_ref[...])
    lane_block = lax.broadcasted_iota(jnp.int32, (SUBLANES, LANES), 1) // SSM_GROUP
    n_tiles = d_model // LANES
    glu_every = n_tiles // (d_model // MXU_DIM)

    def step_rows(c, half, tau, n=1):
        lo = ((c * CHUNK + half * SUBLANES) + tau) * SUBLANES
        return slice(lo, lo + n * SUBLANES)

    def stage_inputs(j):
        lanes = slice(j * LANES, (j + 1) * LANES)
        for cp in range(n_chunks // 2):
            for half in range(2):
                pair = [_block_transpose8([u_ref[step_rows(c, half, tau), lanes] for tau in range(8)],
                                          lane_block) for c in (2 * cp, 2 * cp + 1)]
                for gg in range(tile_groups):
                    xs_ref[j * tile_groups + gg, cp * 16:(cp + 1) * 16, half * LANES:(half + 1) * LANES] = (
                        jnp.concatenate([pair[0][gg], pair[1][gg]], axis=0).astype(BF16))
        return [_dot(xs_ref[j * tile_groups + gg], v_ref[j * tile_groups + gg])
                for gg in range(tile_groups)]

    def chunk_starts(g, s12):
        a_re = jnp.broadcast_to(coef_ref[g, 0:1, :], (SUBLANES, LANES))
        a_im = jnp.broadcast_to(coef_ref[g, 1:2, :], (SUBLANES, LANES))
        a_im_neg = jnp.broadcast_to(coef_ref[g, 2:3, :], (SUBLANES, LANES))
        x1 = st_ref[g, :, :LANES]
        x2 = st_ref[g, :, LANES:]
        starts = []
        for c in range(n_chunks):
            starts.append(x1)
            r = slice(c * SUBLANES, (c + 1) * SUBLANES)
            x1, x2 = (a_re * x1 + a_im * x2 + s12[r, :LANES],
                      a_re * x2 + a_im_neg * x1 + s12[r, LANES:])
        st_ref[g, :, :LANES] = x1
        st_ref[g, :, LANES:] = x2
        return jnp.concatenate(starts, axis=0).astype(BF16)

    def finish_tile(j, increments):
        lanes = slice(j * LANES, (j + 1) * LANES)
        groups = range(j * tile_groups, (j + 1) * tile_groups)
        hs = [chunk_starts(g, s12) for g, s12 in zip(groups, increments)]
        ys = [_dot(xs_ref[g], w_ref[g]) + _dot(h, z_ref[g]) for g, h in zip(groups, hs)]
        d_tile = d_ref[:, lanes]
        for c in range(n_chunks):
            for half in range(2):
                r = slice(c * SUBLANES, (c + 1) * SUBLANES)
                back = _block_transpose8([y[r, half * LANES:(half + 1) * LANES] for y in ys], lane_block)
                for tau in range(0, 8, 2):
                    two_steps = step_rows(c, half, tau, 2)
                    y = d_tile * u_ref[two_steps, lanes] + jnp.concatenate(back[tau:tau + 2], axis=0)
                    act_ref[slot, two_steps, lanes] = jax.nn.gelu(y).astype(BF16)

    increments = stage_inputs(0)
    for j in range(n_tiles):
        ahead = stage_inputs(j + 1) if j + 1 < n_tiles else None
        if j % glu_every == 0:
            glu_columns(j // glu_every * MXU_DIM)
        finish_tile(j, increments)
        increments = ahead


def _s5_layer(h, g, w_mat, v_mat, z_mat, coef, d_skip, w_glu, layer, *, batch, steps):
    batch_major_in = h.ndim == 3
    d_model = h.shape[-1]
    t_rows = h.size // d_model
    rows = steps * batch
    assert steps % (2 * CHUNK) == 0
    n_blocks = t_rows // rows
    if batch_major_in:
        in_spec = pl.BlockSpec((batch, steps, d_model), lambda i: (0, jnp.minimum(i, n_blocks - 1), 0))
    else:
        in_spec = pl.BlockSpec((rows, d_model), lambda i: (jnp.minimum(i, n_blocks - 1), 0))
    return pl.pallas_call(
        functools.partial(_s5_kernel, batch_major_in=batch_major_in),
        grid=(n_blocks + 1,),
        in_specs=[in_spec, _const_spec((1, d_model)), _const_spec(w_mat.shape),
                  _const_spec(v_mat.shape), _const_spec(z_mat.shape), _const_spec(coef.shape),
                  _const_spec((1, d_model)), _layer_spec(w_glu, layer)],
        out_specs=pl.BlockSpec((rows, d_model), lambda i: (jnp.maximum(i - 1, 0), 0)),
        out_shape=jax.ShapeDtypeStruct((t_rows, d_model), F32),
        scratch_shapes=[pltpu.VMEM((rows, d_model), F32),
                        pltpu.VMEM((d_model // SSM_GROUP, rows // CHUNK, MXU_DIM), BF16),
                        pltpu.VMEM((coef.shape[0], batch, 2 * LANES), F32),
                        pltpu.VMEM((2, rows, d_model), BF16)],
        compiler_params=_params("arbitrary"),
        name="s5_layer",
    )(h, g[None, :], w_mat, v_mat, z_mat, coef, d_skip[None, :], w_glu)


def _conv_kernel(h_ref, g_ref, win_ref, wc_ref, wout_ref, o_ref, ext_ref, *, batch):
    d_model = h_ref.shape[1]
    rows = h_ref.shape[0]
    halo = 2 * batch

    @pl.when(pl.program_id(0) == 0)
    def _():
        ext_ref[0:halo, :] = jnp.zeros((halo, d_model), F32)

    x = h_ref[...]
    xn = _rms(x, g_ref[...]).astype(BF16)
    c_gate = _dot(xn, win_ref[:, d_model:2 * d_model])
    v = _dot(xn, win_ref[:, 2 * d_model:])
    ext_ref[halo:, :] = c_gate * v
    conv = (wc_ref[0:1, :] * ext_ref[0:rows, :]
            + wc_ref[1:2, :] * ext_ref[batch:batch + rows, :]
            + wc_ref[2:3, :] * ext_ref[halo:, :])
    ext_ref[0:halo, :] = ext_ref[rows:, :]
    b_gate = _dot(xn, win_ref[:, :d_model])
    o_ref[...] = x + _dot((b_gate * conv).astype(BF16), wout_ref[...])


def _conv_layer(h, g, w_in, w_conv, w_out, layer, *, batch, rows):
    t_rows, d_model = h.shape
    row_spec = pl.BlockSpec((rows, d_model), lambda i: (i, 0))
    return pl.pallas_call(
        functools.partial(_conv_kernel, batch=batch),
        grid=(t_rows // rows,),
        in_specs=[row_spec, _const_spec((1, d_model)), _layer_spec(w_in, layer),
                  _layer_spec(w_conv, layer), _layer_spec(w_out, layer)],
        out_specs=row_spec,
        out_shape=jax.ShapeDtypeStruct(h.shape, F32),
        scratch_shapes=[pltpu.VMEM((rows + 2 * batch, d_model), F32)],
        compiler_params=_params("arbitrary"),
        name="conv_layer",
    )(h, g[None, :], w_in, w_conv, w_out)


def _ffn_kernel(*refs, final, hidden_chunks, has_mix):
    if has_mix:
        h_ref, mix_ref, *refs = refs
    else:
        h_ref, *refs = refs
    p_ref, gf_ref, win_ref, wout_ref, gp_ref, wgate_ref, wup_ref, gfin_ref, o_ref, hid_ref = refs
    hidden = wout_ref.shape[0]
    x = _to_time_major(h_ref[...]) if len(h_ref.shape) == 3 else h_ref[...]
    if has_mix:
        x = x + mix_ref[...]
    xn = _rms(x, gf_ref[...]).astype(BF16)
    for lo, hi in hidden_chunks:
        gate = _dot(xn, win_ref[:, lo:hi])
        up = _dot(xn, win_ref[:, hidden + lo:hidden + hi])
        hid_ref[:, lo:hi] = (gate * jax.nn.sigmoid(gate) * up).astype(BF16)
    y = x + _dot(hid_ref[...], wout_ref[...])
    yn = _rms(y, gp_ref[...]).astype(BF16)
    ple_gate = jax.nn.sigmoid(_dot(yn, wgate_ref[...]))
    p_rows = _to_time_major(p_ref[...]).astype(BF16)
    out = y + ple_gate * _dot(p_rows, wup_ref[...])
    if final:
        o_ref[...] = _to_batch_major(_rms(out, gfin_ref[...]), p_ref.shape[0])
    else:
        o_ref[...] = out


def _chunks(total, size):
    return tuple((lo, min(lo + size, total)) for lo in range(0, total, size))


def _ffn_layer(h, mix, p, g_ffn, w_in, w_out, g_ple, w_gate, w_up, g_final, layer, *, steps, final):
    _, batch, seq, ple = p.shape
    d_model = h.shape[-1]
    t_rows = seq * batch
    hidden = w_out.shape[1]
    rows = steps * batch
    row_spec = pl.BlockSpec((rows, d_model), lambda i: (i, 0))
    batch_spec = pl.BlockSpec((batch, steps, d_model), lambda i: (0, i, 0))
    if final:
        out_spec, out_shape = batch_spec, jax.ShapeDtypeStruct((batch, seq, d_model), F32)
    else:
        out_spec, out_shape = row_spec, jax.ShapeDtypeStruct((t_rows, d_model), F32)
    acts = [h] if mix is None else [h, mix]
    act_specs = [batch_spec if h.ndim == 3 else row_spec] + ([] if mix is None else [row_spec])
    kernel = functools.partial(_ffn_kernel, final=final, has_mix=mix is not None,
                               hidden_chunks=_chunks(hidden, 4 * MXU_DIM))
    return pl.pallas_call(
        kernel,
        grid=(t_rows // rows,),
        in_specs=act_specs + [
            pl.BlockSpec((None, batch, steps, ple), lambda i: (layer, 0, i, 0)),
            _const_spec((1, d_model)), _layer_spec(w_in, layer), _layer_spec(w_out, layer),
            _const_spec((1, d_model)), _layer_spec(w_gate, layer), _layer_spec(w_up, layer),
            _const_spec((1, d_model))],
        out_specs=out_spec,
        out_shape=out_shape,
        scratch_shapes=[pltpu.VMEM((rows, hidden), BF16)],
        compiler_params=_params("parallel"),
        name="ffn_layer",
    )(*acts, p, g_ffn[None, :], w_in, w_out, g_ple[None, :], w_gate, w_up, g_final[None, :])


def kernel(x, p, norm_mix_g, s5_a_re, s5_a_im, s5_log_dt, s5_b_re, s5_b_im, s5_c_re, s5_c_im, s5_d, s5_w_glu, conv_w_in, conv_w, conv_w_out, norm_ffn_g, ffn_w_in, ffn_w_out, norm_ple_g, ple_w_gate, ple_w_up, final_norm_g):
    batch, seq, d_model = x.shape
    depth = p.shape[0]
    assert batch == SUBLANES and d_model % MXU_DIM == 0
    steps = min(64, seq)

    s5_w_glu, conv_w_in, conv_w_out, ffn_w_in, ffn_w_out, ple_w_gate, ple_w_up = (
        w.astype(BF16) for w in (s5_w_glu, conv_w_in, conv_w_out, ffn_w_in, ffn_w_out,
                                 ple_w_gate, ple_w_up))
    h = x
    for i in range(depth):
        j = i // 2
        if i % 2 == 0:
            mats = _s5_prep(s5_a_re[j], s5_a_im[j], s5_log_dt[j], s5_b_re[j], s5_b_im[j],
                            s5_c_re[j], s5_c_im[j])
            mix = _s5_layer(h, norm_mix_g[i], *mats, s5_d[j], s5_w_glu, j,
                            batch=batch, steps=min(2 * steps, seq))
        else:
            h = _conv_layer(h, norm_mix_g[i], conv_w_in, conv_w, conv_w_out, j,
                            batch=batch, rows=steps * batch)
            mix = None
        h = _ffn_layer(h, mix, p, norm_ffn_g[i], ffn_w_in, ffn_w_out, norm_ple_g[i], ple_w_gate,
                       ple_w_up, final_norm_g, i, steps=steps, final=(i == depth - 1))
    return h
```

```python
import functools

import jax
import jax.numpy as jnp
from jax import lax
from jax.experimental import pallas as pl
from jax.experimental.pallas import tpu as pltpu

EPS = 1e-6
SSM_GROUP = 16
LANES = 128
SUBLANES = 8
MXU_DIM = 256
CHUNK = MXU_DIM // SSM_GROUP
VMEM_LIMIT = 56 * 1024 * 1024

BF16 = jnp.bfloat16
F32 = jnp.float32


def _rms(x, g):
    return x * lax.rsqrt(jnp.mean(x * x, axis=-1, keepdims=True) + EPS) * g


def _dot(a, b):
    return jnp.dot(a, b, preferred_element_type=F32)


def _const_spec(shape):
    zeros = (0,) * len(shape)
    return pl.BlockSpec(shape, lambda i: zeros, pipeline_mode=pl.Buffered(1))


def _layer_spec(stacked, layer):
    tail = (0,) * (stacked.ndim - 1)
    return pl.BlockSpec((None,) + stacked.shape[1:], lambda i: (layer,) + tail,
                        pipeline_mode=pl.Buffered(1))


def _params(sem):
    return pltpu.CompilerParams(dimension_semantics=(sem,), vmem_limit_bytes=VMEM_LIMIT)


def _to_time_major(blk):
    b, s, d = blk.shape
    return jnp.swapaxes(blk, 0, 1).reshape(s * b, d)


def _to_batch_major(rows, batch):
    n, d = rows.shape
    return jnp.swapaxes(rows.reshape(n // batch, batch, d), 0, 1)


def _s5_prep_kernel(a_re_ref, a_im_ref, log_dt_ref, b_re_ref, b_im_ref, c_re_ref, c_im_ref,
                    w_ref, v_ref, z_ref, coef_ref):
    n_groups, k, p = b_re_ref.shape
    width = CHUNK * k
    first_half = lax.broadcasted_iota(jnp.int32, (1, 2 * p), 1) < p
    neg_first = jnp.where(first_half, -1.0, 1.0)
    lane_block = lax.broadcasted_iota(jnp.int32, (width, width), 1) // k

    def swap(v):
        return pltpu.roll(v, p, axis=1)

    def twice(v):
        return jnp.concatenate([v, v], axis=-1)

    for i in range(n_groups):
        a_re = a_re_ref[i]
        a_im = a_im_ref[i]
        dt = jnp.exp(log_dt_ref[i])
        mag = jnp.exp(dt * a_re)
        ang = dt * a_im
        abar_re = mag * jnp.cos(ang)
        abar_im = mag * jnp.sin(ang)
        nr = abar_re - 1.0
        ni = abar_im
        den = a_re * a_re + a_im * a_im
        f_re = (nr * a_re + ni * a_im) / den
        f_im = (ni * a_re - nr * a_im) / den
        b_re = b_re_ref[i]
        b_im = b_im_ref[i]
        bb_re = twice(f_re * b_re - f_im * b_im)
        bb_im = twice(f_re * b_im + f_im * b_re)
        c_re = c_re_ref[i]
        c_im = c_im_ref[i]

        mul_re = twice(abar_re)
        mul_im = jnp.concatenate([-abar_im, abar_im], axis=-1)
        power = jnp.where(first_half, 1.0, 0.0)
        flipped = 1.0 - power
        powers, swapped = [power], [flipped]
        for _ in range(CHUNK):
            power, flipped = (power * mul_re + flipped * mul_im,
                              flipped * mul_re - power * mul_im)
            powers.append(power)
            swapped.append(flipped)

        bp = jnp.concatenate(
            [bb_re * powers[CHUNK - 1 - rho] + bb_im * (swapped[CHUNK - 1 - rho] * neg_first)
             for rho in range(CHUNK)], axis=0)
        cc = jnp.concatenate([jnp.concatenate([c_re, -c_im], axis=-1)] * CHUNK, axis=0)
        gx = lax.dot_general(bp, cc, (((1,), (1,)), ((), ())), precision=lax.Precision.HIGHEST,
                             preferred_element_type=F32)
        w = jnp.zeros((width, width), F32)
        for t in range(CHUNK):
            up = k * (CHUNK - 1 - t)
            shifted = gx if up == 0 else jnp.concatenate(
                [gx[up:, :], jnp.zeros((up, width), F32)], axis=0)
            w = jnp.where(lane_block == t, shifted, w)
        w_ref[i] = w.astype(BF16)
        v_ref[i] = jnp.concatenate([bp, swap(bp)], axis=-1).astype(BF16)

        cr2 = twice(c_re)
        ci2 = twice(c_im)
        zt = jnp.concatenate(
            [cr2 * (powers[t + 1] * -neg_first) - ci2 * swapped[t + 1]
             for t in range(CHUNK)], axis=0)
        z_ref[i] = zt.T.astype(BF16)

        last, last_flipped = powers[CHUNK], swapped[CHUNK]
        re2 = jnp.where(first_half, last, last_flipped)
        im2 = jnp.where(first_half, last_flipped, last) * neg_first
        coef_ref[i] = jnp.concatenate(
            [re2, im2, -im2, jnp.zeros((SUBLANES - 3, 2 * p), F32)], axis=0)


def _s5_prep(a_re, a_im, log_dt, b_re, b_im, c_re, c_im):
    g, p = a_re.shape
    k = b_re.shape[-1]
    assert CHUNK * k == MXU_DIM and 2 * p == LANES
    per_step = SUBLANES
    width = CHUNK * k

    def spec(*tail):
        return pl.BlockSpec((per_step,) + tail, lambda i: (i,) + (0,) * len(tail))

    return pl.pallas_call(
        _s5_prep_kernel,
        grid=(g // per_step,),
        in_specs=[spec(1, p), spec(1, p), spec(1, 1), spec(k, p), spec(k, p), spec(k, p), spec(k, p)],
        out_specs=(spec(width, width), spec(width, 2 * LANES), spec(LANES, width), spec(SUBLANES, LANES)),
        out_shape=(jax.ShapeDtypeStruct((g, width, width), BF16),
                   jax.ShapeDtypeStruct((g, width, 2 * LANES), BF16),
                   jax.ShapeDtypeStruct((g, LANES, width), BF16),
                   jax.ShapeDtypeStruct((g, SUBLANES, LANES), F32)),
        compiler_params=_params("parallel"),
        name="s5_prep",
    )(a_re[:, None, :], a_im[:, None, :], log_dt[:, None, None], jnp.swapaxes(b_re, 1, 2),
      jnp.swapaxes(b_im, 1, 2), c_re, c_im)


def _block_transpose8(tiles, lane_block):
    v = list(tiles)
    for d in (4, 2, 1):
        high = (lane_block & d) != 0
        for i in range(8):
            if i & d:
                continue
            lo, hi = v[i], v[i + d]
            v[i] = jnp.where(high, pltpu.roll(hi, SSM_GROUP * d, axis=1), lo)
            v[i + d] = jnp.where(high, hi, pltpu.roll(lo, LANES - SSM_GROUP * d, axis=1))
    return v


def _s5_kernel(h_ref, g_ref, w_ref, v_ref, z_ref, coef_ref, d_ref, wglu_ref,
               o_ref, u_ref, xs_ref, st_ref, act_ref, *, batch_major_in):
    rows, d_model = o_ref.shape
    n_chunks = rows // (CHUNK * SUBLANES)
    tile_groups = LANES // SSM_GROUP
    slot = pl.program_id(0) % 2

    @pl.when(pl.program_id(0) == 0)
    def _():
        st_ref[...] = jnp.zeros_like(st_ref)
        act_ref[...] = jnp.zeros_like(act_ref)

    prev = act_ref[1 - slot]

    def glu_columns(lo):
        val = _dot(prev, wglu_ref[:, lo:lo + MXU_DIM])
        gate = _dot(prev, wglu_ref[:, d_model + lo:d_model + lo + MXU_DIM])
        o_ref[:, lo:lo + MXU_DIM] = val * jax.nn.sigmoid(gate)

    x = _to_time_major(h_ref[...]) if batch_major_in else h_ref[...]
    u_ref[...] = _rms(x, g_ref[...])
    lane_block = lax.broadcasted_iota(jnp.int32, (SUBLANES, LANES), 1) // SSM_GROUP
    n_tiles = d_model // LANES
    glu_every = n_tiles // (d_model // MXU_DIM)

    def step_rows(c, half, tau, n=1):
        lo = ((c * CHUNK + half * SUBLANES) + tau) * SUBLANES
        return slice(lo, lo + n * SUBLANES)

    def stage_inputs(j):
        lanes = slice(j * LANES, (j + 1) * LANES)
        for cp in range(n_chunks // 2):
            for half in range(2):
                pair = [_block_transpose8([u_ref[step_rows(c, half, tau), lanes] for tau in range(8)],
                                          lane_block) for c in (2 * cp, 2 * cp + 1)]
                for gg in range(tile_groups):
                    xs_ref[j * tile_groups + gg, cp * 16:(cp + 1) * 16, half * LANES:(half + 1) * LANES] = (
                        jnp.concatenate([pair[0][gg], pair[1][gg]], axis=0).astype(BF16))
        return [_dot(xs_ref[j * tile_groups + gg], v_ref[j * tile_groups + gg])
                for gg in range(tile_groups)]

    def chunk_starts(g, s12):
        a_re = jnp.broadcast_to(coef_ref[g, 0:1, :], (SUBLANES, LANES))
        a_im = jnp.broadcast_to(coef_ref[g, 1:2, :], (SUBLANES, LANES))
        a_im_neg = jnp.broadcast_to(coef_ref[g, 2:3, :], (SUBLANES, LANES))
        x1 = st_ref[g, :, :LANES]
        x2 = st_ref[g, :, LANES:]
        starts = []
        for c in range(n_chunks):
            starts.append(x1)
            r = slice(c * SUBLANES, (c + 1) * SUBLANES)
            x1, x2 = (a_re * x1 + a_im * x2 + s12[r, :LANES],
                      a_re * x2 + a_im_neg * x1 + s12[r, LANES:])
        st_ref[g, :, :LANES] = x1
        st_ref[g, :, LANES:] = x2
        return jnp.concatenate(starts, axis=0).astype(BF16)

    def finish_tile(j, increments):
        lanes = slice(j * LANES, (j + 1) * LANES)
        groups = range(j * tile_groups, (j + 1) * tile_groups)
        hs = [chunk_starts(g, s12) for g, s12 in zip(groups, increments)]
        ys = [_dot(xs_ref[g], w_ref[g]) + _dot(h, z_ref[g]) for g, h in zip(groups, hs)]
        d_tile = d_ref[:, lanes]
        for c in range(n_chunks):
            for half in range(2):
                r = slice(c * SUBLANES, (c + 1) * SUBLANES)
                back = _block_transpose8([y[r, half * LANES:(half + 1) * LANES] for y in ys], lane_block)
                for tau in range(0, 8, 2):
                    two_steps = step_rows(c, half, tau, 2)
                    y = d_tile * u_ref[two_steps, lanes] + jnp.concatenate(back[tau:tau + 2], axis=0)
                    act_ref[slot, two_steps, lanes] = jax.nn.gelu(y).astype(BF16)

    increments = stage_inputs(0)
    for j in range(n_tiles):
        ahead = stage_inputs(j + 1) if j + 1 < n_tiles else None
        if j % glu_every == 0:
            glu_columns(j // glu_every * MXU_DIM)
        finish_tile(j, increments)
        increments = ahead


def _s5_layer(h, g, w_mat, v_mat, z_mat, coef, d_skip, w_glu, layer, *, batch, steps):
    batch_major_in = h.ndim == 3
    d_model = h.shape[-1]
    t_rows = h.size // d_model
    rows = steps * batch
    assert steps % (2 * CHUNK) == 0
    n_blocks = t_rows // rows
    if batch_major_in:
        in_spec = pl.BlockSpec((batch, steps, d_model), lambda i: (0, jnp.minimum(i, n_blocks - 1), 0))
    else:
        in_spec = pl.BlockSpec((rows, d_model), lambda i: (jnp.minimum(i, n_blocks - 1), 0))
    return pl.pallas_call(
        functools.partial(_s5_kernel, batch_major_in=batch_major_in),
        grid=(n_blocks + 1,),
        in_specs=[in_spec, _const_spec((1, d_model)), _const_spec(w_mat.shape),
                  _const_spec(v_mat.shape), _const_spec(z_mat.shape), _const_spec(coef.shape),
                  _const_spec((1, d_model)), _layer_spec(w_glu, layer)],
        out_specs=pl.BlockSpec((rows, d_model), lambda i: (jnp.maximum(i - 1, 0), 0)),
        out_shape=jax.ShapeDtypeStruct((t_rows, d_model), F32),
        scratch_shapes=[pltpu.VMEM((rows, d_model), F32),
                        pltpu.VMEM((d_model // SSM_GROUP, rows // CHUNK, MXU_DIM), BF16),
                        pltpu.VMEM((coef.shape[0], batch, 2 * LANES), F32),
                        pltpu.VMEM((2, rows, d_model), BF16)],
        compiler_params=_params("arbitrary"),
        name="s5_layer",
    )(h, g[None, :], w_mat, v_mat, z_mat, coef, d_skip[None, :], w_glu)


def _conv_kernel(h_ref, g_ref, win_ref, wc_ref, wout_ref, o_ref, ext_ref, *, batch):
    d_model = h_ref.shape[1]
    rows = h_ref.shape[0]
    halo = 2 * batch

    @pl.when(pl.program_id(0) == 0)
    def _():
        ext_ref[0:halo, :] = jnp.zeros((halo, d_model), F32)

    x = h_ref[...]
    xn = _rms(x, g_ref[...]).astype(BF16)
    c_gate = _dot(xn, win_ref[:, d_model:2 * d_model])
    v = _dot(xn, win_ref[:, 2 * d_model:])
    ext_ref[halo:, :] = c_gate * v
    conv = (wc_ref[0:1, :] * ext_ref[0:rows, :]
            + wc_ref[1:2, :] * ext_ref[batch:batch + rows, :]
            + wc_ref[2:3, :] * ext_ref[halo:, :])
    ext_ref[0:halo, :] = ext_ref[rows:, :]
    b_gate = _dot(xn, win_ref[:, :d_model])
    o_ref[...] = x + _dot((b_gate * conv).astype(BF16), wout_ref[...])


def _conv_layer(h, g, w_in, w_conv, w_out, layer, *, batch, rows):
    t_rows, d_model = h.shape
    row_spec = pl.BlockSpec((rows, d_model), lambda i: (i, 0))
    return pl.pallas_call(
        functools.partial(_conv_kernel, batch=batch),
        grid=(t_rows // rows,),
        in_specs=[row_spec, _const_spec((1, d_model)), _layer_spec(w_in, layer),
                  _layer_spec(w_conv, layer), _layer_spec(w_out, layer)],
        out_specs=row_spec,
        out_shape=jax.ShapeDtypeStruct(h.shape, F32),
        scratch_shapes=[pltpu.VMEM((rows + 2 * batch, d_model), F32)],
        compiler_params=_params("arbitrary"),
        name="conv_layer",
    )(h, g[None, :], w_in, w_conv, w_out)


def _ffn_kernel(*refs, final, hidden_chunks, has_mix, row_parts):
    if has_mix:
        h_ref, mix_ref, *refs = refs
    else:
        h_ref, *refs = refs
    p_ref, gf_ref, win_ref, wout_ref, gp_ref, wgate_ref, wup_ref, gfin_ref, o_ref, hid_ref = refs
    hidden = wout_ref.shape[0]
    batch, steps, _ = p_ref.shape
    span = steps // row_parts
    times = [(k * span, (k + 1) * span) for k in range(row_parts)]

    def hidden_stage(t0, t1):
        r = slice(t0 * batch, t1 * batch)
        x = _to_time_major(h_ref[:, t0:t1, :]) if len(h_ref.shape) == 3 else h_ref[r, :]
        if has_mix:
            x = x + mix_ref[r, :]
        xn = _rms(x, gf_ref[...]).astype(BF16)
        for lo, hi in hidden_chunks:
            gate = _dot(xn, win_ref[:, lo:hi])
            up = _dot(xn, win_ref[:, hidden + lo:hidden + hi])
            hid_ref[r, lo:hi] = (gate * jax.nn.sigmoid(gate) * up).astype(BF16)
        return x

    def out_stage(t0, t1, x):
        r = slice(t0 * batch, t1 * batch)
        y = x + _dot(hid_ref[r, :], wout_ref[...])
        ple_up = _dot(_to_time_major(p_ref[:, t0:t1, :]).astype(BF16), wup_ref[...])
        return y, ple_up

    def ple_stage(t0, t1, y, ple_up):
        yn = _rms(y, gp_ref[...]).astype(BF16)
        out = y + jax.nn.sigmoid(_dot(yn, wgate_ref[...])) * ple_up
        if final:
            o_ref[:, t0:t1, :] = _to_batch_major(_rms(out, gfin_ref[...]), batch)
        else:
            o_ref[t0 * batch:t1 * batch, :] = out

    xs = [hidden_stage(*times[0])]
    pending = None
    for k in range(row_parts):
        ys = out_stage(*times[k], xs[k])
        if k + 1 < row_parts:
            xs.append(hidden_stage(*times[k + 1]))
        if pending is not None:
            ple_stage(*pending)
        pending = (*times[k], *ys)
    ple_stage(*pending)


def _chunks(total, size):
    return tuple((lo, min(lo + size, total)) for lo in range(0, total, size))


def _ffn_layer(h, mix, p, g_ffn, w_in, w_out, g_ple, w_gate, w_up, g_final, layer, *, steps, final):
    _, batch, seq, ple = p.shape
    d_model = h.shape[-1]
    t_rows = seq * batch
    hidden = w_out.shape[1]
    rows = steps * batch
    row_spec = pl.BlockSpec((rows, d_model), lambda i: (i, 0))
    batch_spec = pl.BlockSpec((batch, steps, d_model), lambda i: (0, i, 0))
    if final:
        out_spec, out_shape = batch_spec, jax.ShapeDtypeStruct((batch, seq, d_model), F32)
    else:
        out_spec, out_shape = row_spec, jax.ShapeDtypeStruct((t_rows, d_model), F32)
    acts = [h] if mix is None else [h, mix]
    act_specs = [batch_spec if h.ndim == 3 else row_spec] + ([] if mix is None else [row_spec])
    kernel = functools.partial(_ffn_kernel, final=final, has_mix=mix is not None,
                               hidden_chunks=_chunks(hidden, 4 * MXU_DIM),
                               row_parts=2 if steps % (2 * SUBLANES) == 0 else 1)
    return pl.pallas_call(
        kernel,
        grid=(t_rows // rows,),
        in_specs=act_specs + [
            pl.BlockSpec((None, batch, steps, ple), lambda i: (layer, 0, i, 0)),
            _const_spec((1, d_model)), _layer_spec(w_in, layer), _layer_spec(w_out, layer),
            _const_spec((1, d_model)), _layer_spec(w_gate, layer), _layer_spec(w_up, layer),
            _const_spec((1, d_model))],
        out_specs=out_spec,
        out_shape=out_shape,
        scratch_shapes=[pltpu.VMEM((rows, hidden), BF16)],
        compiler_params=_params("parallel"),
        name="ffn_layer",
    )(*acts, p, g_ffn[None, :], w_in, w_out, g_ple[None, :], w_gate, w_up, g_final[None, :])


def kernel(x, p, norm_mix_g, s5_a_re, s5_a_im, s5_log_dt, s5_b_re, s5_b_im, s5_c_re, s5_c_im, s5_d, s5_w_glu, conv_w_in, conv_w, conv_w_out, norm_ffn_g, ffn_w_in, ffn_w_out, norm_ple_g, ple_w_gate, ple_w_up, final_norm_g):
    batch, seq, d_model = x.shape
    depth = p.shape[0]
    assert batch == SUBLANES and d_model % MXU_DIM == 0
    steps = min(64, seq)

    s5_w_glu, conv_w_in, conv_w_out, ffn_w_in, ffn_w_out, ple_w_gate, ple_w_up = (
        w.astype(BF16) for w in (s5_w_glu, conv_w_in, conv_w_out, ffn_w_in, ffn_w_out,
                                 ple_w_gate, ple_w_up))
    h = x
    for i in range(depth):
        j = i // 2
        if i % 2 == 0:
            mats = _s5_prep(s5_a_re[j], s5_a_im[j], s5_log_dt[j], s5_b_re[j], s5_b_im[j],
                            s5_c_re[j], s5_c_im[j])
            mix = _s5_layer(h, norm_mix_g[i], *mats, s5_d[j], s5_w_glu, j,
                            batch=batch, steps=min(2 * steps, seq))
        else:
            h = _conv_layer(h, norm_mix_g[i], conv_w_in, conv_w, conv_w_out, j,
                            batch=batch, rows=steps * batch)
            mix = None
        h = _ffn_layer(h, mix, p, norm_ffn_g[i], ffn_w_in, ffn_w_out, norm_ple_g[i], ple_w_gate,
                       ple_w_up, final_norm_g, i, steps=steps, final=(i == depth - 1))
    return h
```

```python
import functools

import jax
import jax.numpy as jnp
from jax import lax
from jax.experimental import pallas as pl
from jax.experimental.pallas import tpu as pltpu

EPS = 1e-6
SSM_GROUP = 16
LANES = 128
SUBLANES = 8
MXU_DIM = 256
CHUNK = MXU_DIM // SSM_GROUP
TILE_GROUPS = LANES // SSM_GROUP
BF16_ROWS = 2 * SUBLANES
VMEM_LIMIT = 56 * 1024 * 1024

BF16 = jnp.bfloat16
F32 = jnp.float32


def _rms(x, g):
    return x * lax.rsqrt(jnp.mean(x * x, axis=-1, keepdims=True) + EPS) * g


def _dot(a, b):
    return jnp.dot(a, b, preferred_element_type=F32)


def _const_spec(shape):
    zeros = (0,) * len(shape)
    return pl.BlockSpec(shape, lambda i: zeros, pipeline_mode=pl.Buffered(1))


def _layer_spec(stacked, layer):
    tail = (0,) * (stacked.ndim - 1)
    return pl.BlockSpec((None,) + stacked.shape[1:], lambda i: (layer,) + tail,
                        pipeline_mode=pl.Buffered(1))


def _params(sem):
    return pltpu.CompilerParams(dimension_semantics=(sem,), vmem_limit_bytes=VMEM_LIMIT)


def _to_time_major(blk):
    b, s, d = blk.shape
    return jnp.swapaxes(blk, 0, 1).reshape(s * b, d)


def _to_batch_major(rows, batch):
    n, d = rows.shape
    return jnp.swapaxes(rows.reshape(n // batch, batch, d), 0, 1)


def _s5_prep_kernel(a_re_ref, a_im_ref, log_dt_ref, b_re_ref, b_im_ref, c_re_ref, c_im_ref,
                    w_ref, v_ref, z_ref, coef_ref):
    n_groups, k, p = b_re_ref.shape
    width = CHUNK * k
    first_half = lax.broadcasted_iota(jnp.int32, (1, 2 * p), 1) < p
    neg_first = jnp.where(first_half, -1.0, 1.0)
    lane_block = lax.broadcasted_iota(jnp.int32, (width, width), 1) // k

    def swap(v):
        return pltpu.roll(v, p, axis=1)

    def twice(v):
        return jnp.concatenate([v, v], axis=-1)

    for i in range(n_groups):
        a_re = a_re_ref[i]
        a_im = a_im_ref[i]
        dt = jnp.exp(log_dt_ref[i])
        mag = jnp.exp(dt * a_re)
        ang = dt * a_im
        abar_re = mag * jnp.cos(ang)
        abar_im = mag * jnp.sin(ang)
        nr = abar_re - 1.0
        ni = abar_im
        den = a_re * a_re + a_im * a_im
        f_re = (nr * a_re + ni * a_im) / den
        f_im = (ni * a_re - nr * a_im) / den
        b_re = b_re_ref[i]
        b_im = b_im_ref[i]
        bb_re = twice(f_re * b_re - f_im * b_im)
        bb_im = twice(f_re * b_im + f_im * b_re)
        c_re = c_re_ref[i]
        c_im = c_im_ref[i]

        mul_re = twice(abar_re)
        mul_im = jnp.concatenate([-abar_im, abar_im], axis=-1)
        power = jnp.where(first_half, 1.0, 0.0)
        flipped = 1.0 - power
        powers, swapped = [power], [flipped]
        for _ in range(CHUNK):
            power, flipped = (power * mul_re + flipped * mul_im,
                              flipped * mul_re - power * mul_im)
            powers.append(power)
            swapped.append(flipped)

        bp = jnp.concatenate(
            [bb_re * powers[CHUNK - 1 - rho] + bb_im * (swapped[CHUNK - 1 - rho] * neg_first)
             for rho in range(CHUNK)], axis=0)
        cc = jnp.concatenate([jnp.concatenate([c_re, -c_im], axis=-1)] * CHUNK, axis=0)
        gx = lax.dot_general(bp, cc, (((1,), (1,)), ((), ())), precision=lax.Precision.HIGHEST,
                             preferred_element_type=F32)
        w = jnp.zeros((width, width), F32)
        for t in range(CHUNK):
            up = k * (CHUNK - 1 - t)
            shifted = gx if up == 0 else jnp.concatenate(
                [gx[up:, :], jnp.zeros((up, width), F32)], axis=0)
            w = jnp.where(lane_block == t, shifted, w)
        w_ref[i] = w.astype(BF16)
        v_ref[i] = jnp.concatenate([bp, swap(bp)], axis=-1).astype(BF16)

        cr2 = twice(c_re)
        ci2 = twice(c_im)
        zt = jnp.concatenate(
            [cr2 * (powers[t + 1] * -neg_first) - ci2 * swapped[t + 1]
             for t in range(CHUNK)], axis=0)
        z_ref[i] = zt.T.astype(BF16)

        last, last_flipped = powers[CHUNK], swapped[CHUNK]
        re2 = jnp.where(first_half, last, last_flipped)
        im2 = jnp.where(first_half, last_flipped, last) * neg_first
        coef_ref[i] = jnp.concatenate(
            [re2, im2, -im2, jnp.zeros((SUBLANES - 3, 2 * p), F32)], axis=0)


def _s5_prep(a_re, a_im, log_dt, b_re, b_im, c_re, c_im):
    g, p = a_re.shape
    k = b_re.shape[-1]
    assert CHUNK * k == MXU_DIM and 2 * p == LANES
    per_step = SUBLANES
    width = CHUNK * k

    def spec(*tail):
        return pl.BlockSpec((per_step,) + tail, lambda i: (i,) + (0,) * len(tail))

    return pl.pallas_call(
        _s5_prep_kernel,
        grid=(g // per_step,),
        in_specs=[spec(1, p), spec(1, p), spec(1, 1), spec(k, p), spec(k, p), spec(k, p), spec(k, p)],
        out_specs=(spec(width, width), spec(width, 2 * LANES), spec(LANES, width), spec(SUBLANES, LANES)),
        out_shape=(jax.ShapeDtypeStruct((g, width, width), BF16),
                   jax.ShapeDtypeStruct((g, width, 2 * LANES), BF16),
                   jax.ShapeDtypeStruct((g, LANES, width), BF16),
                   jax.ShapeDtypeStruct((g, SUBLANES, LANES), F32)),
        compiler_params=_params("parallel"),
        name="s5_prep",
    )(a_re[:, None, :], a_im[:, None, :], log_dt[:, None, None], jnp.swapaxes(b_re, 1, 2),
      jnp.swapaxes(b_im, 1, 2), c_re, c_im)


def _block_transpose8(tiles, lane_block):
    v = list(tiles)
    assert len(v) == TILE_GROUPS == 8
    for d in (4, 2, 1):
        high = (lane_block & d) != 0
        for i in range(TILE_GROUPS):
            if i & d:
                continue
            lo, hi = v[i], v[i + d]
            v[i] = jnp.where(high, pltpu.roll(hi, SSM_GROUP * d, axis=1), lo)
            v[i + d] = jnp.where(high, hi, pltpu.roll(lo, LANES - SSM_GROUP * d, axis=1))
    return v


def _s5_kernel(h_ref, g_ref, w_ref, v_ref, z_ref, coef_ref, d_ref, wglu_ref,
               o_ref, u_ref, xs_ref, st_ref, act_ref, *, batch_major_in):
    rows, d_model = o_ref.shape
    n_chunks = rows // (CHUNK * SUBLANES)
    tile_groups = TILE_GROUPS
    slot = pl.program_id(0) % 2

    @pl.when(pl.program_id(0) == 0)
    def _():
        st_ref[...] = jnp.zeros_like(st_ref)
        act_ref[...] = jnp.zeros_like(act_ref)

    prev = act_ref[1 - slot]

    def glu_columns(lo):
        val = _dot(prev, wglu_ref[:, lo:lo + MXU_DIM])
        gate = _dot(prev, wglu_ref[:, d_model + lo:d_model + lo + MXU_DIM])
        o_ref[:, lo:lo + MXU_DIM] = val * jax.nn.sigmoid(gate)

    x = _to_time_major(h_ref[...]) if batch_major_in else h_ref[...]
    u_ref[...] = _rms(x, g_ref[...])
    lane_block = lax.broadcasted_iota(jnp.int32, (SUBLANES, LANES), 1) // SSM_GROUP
    n_tiles = d_model // LANES
    glu_every = n_tiles // (d_model // MXU_DIM)

    def step_rows(c, half, tau, n=1):
        lo = ((c * CHUNK + half * SUBLANES) + tau) * SUBLANES
        return slice(lo, lo + n * SUBLANES)

    def stage_inputs(j):
        lanes = slice(j * LANES, (j + 1) * LANES)
        for cp in range(n_chunks // 2):
            for half in range(2):
                packed = [pltpu.bitcast(jnp.concatenate(
                    [u_ref[step_rows(2 * cp, half, tau), lanes],
                     u_ref[step_rows(2 * cp + 1, half, tau), lanes]], axis=0).astype(BF16), jnp.uint32)
                    for tau in range(TILE_GROUPS)]
                moved = _block_transpose8(packed, lax.broadcasted_iota(
                    jnp.int32, packed[0].shape, 1) // SSM_GROUP)
                for gg in range(tile_groups):
                    xs_ref[j * tile_groups + gg, cp * BF16_ROWS:(cp + 1) * BF16_ROWS,
                           half * LANES:(half + 1) * LANES] = (
                        pltpu.bitcast(moved[gg], BF16))
        return [_dot(xs_ref[j * tile_groups + gg], v_ref[j * tile_groups + gg])
                for gg in range(tile_groups)]

    def chunk_starts(g, s12):
        a_re = jnp.broadcast_to(coef_ref[g, 0:1, :], (SUBLANES, LANES))
        a_im = jnp.broadcast_to(coef_ref[g, 1:2, :], (SUBLANES, LANES))
        a_im_neg = jnp.broadcast_to(coef_ref[g, 2:3, :], (SUBLANES, LANES))
        x1 = st_ref[g, :, :LANES]
        x2 = st_ref[g, :, LANES:]
        starts = []
        for c in range(n_chunks):
            starts.append(x1)
            r = slice(c * SUBLANES, (c + 1) * SUBLANES)
            x1, x2 = (a_re * x1 + a_im * x2 + s12[r, :LANES],
                      a_re * x2 + a_im_neg * x1 + s12[r, LANES:])
        st_ref[g, :, :LANES] = x1
        st_ref[g, :, LANES:] = x2
        return jnp.concatenate(starts, axis=0).astype(BF16)

    def finish_tile(j, increments):
        lanes = slice(j * LANES, (j + 1) * LANES)
        groups = range(j * tile_groups, (j + 1) * tile_groups)
        hs = [chunk_starts(g, s12) for g, s12 in zip(groups, increments)]
        ys = [_dot(xs_ref[g], w_ref[g]) + _dot(h, z_ref[g]) for g, h in zip(groups, hs)]
        d_tile = d_ref[:, lanes]
        for c in range(n_chunks):
            for half in range(2):
                r = slice(c * SUBLANES, (c + 1) * SUBLANES)
                back = _block_transpose8([y[r, half * LANES:(half + 1) * LANES] for y in ys], lane_block)
                for tau in range(0, TILE_GROUPS, 2):
                    two_steps = step_rows(c, half, tau, 2)
                    y = d_tile * u_ref[two_steps, lanes] + jnp.concatenate(back[tau:tau + 2], axis=0)
                    act_ref[slot, two_steps, lanes] = jax.nn.gelu(y).astype(BF16)

    increments = stage_inputs(0)
    for j in range(n_tiles):
        ahead = stage_inputs(j + 1) if j + 1 < n_tiles else None
        if j % glu_every == 0:
            glu_columns(j // glu_every * MXU_DIM)
        finish_tile(j, increments)
        increments = ahead


def _s5_layer(h, g, w_mat, v_mat, z_mat, coef, d_skip, w_glu, layer, *, batch, steps):
    batch_major_in = h.ndim == 3
    d_model = h.shape[-1]
    t_rows = h.size // d_model
    rows = steps * batch
    assert steps % (2 * CHUNK) == 0
    n_blocks = t_rows // rows
    if batch_major_in:
        in_spec = pl.BlockSpec((batch, steps, d_model), lambda i: (0, jnp.minimum(i, n_blocks - 1), 0))
    else:
        in_spec = pl.BlockSpec((rows, d_model), lambda i: (jnp.minimum(i, n_blocks - 1), 0))
    return pl.pallas_call(
        functools.partial(_s5_kernel, batch_major_in=batch_major_in),
        grid=(n_blocks + 1,),
        in_specs=[in_spec, _const_spec((1, d_model)), _const_spec(w_mat.shape),
                  _const_spec(v_mat.shape), _const_spec(z_mat.shape), _const_spec(coef.shape),
                  _const_spec((1, d_model)), _layer_spec(w_glu, layer)],
        out_specs=pl.BlockSpec((rows, d_model), lambda i: (jnp.maximum(i - 1, 0), 0)),
        out_shape=jax.ShapeDtypeStruct((t_rows, d_model), F32),
        scratch_shapes=[pltpu.VMEM((rows, d_model), F32),
                        pltpu.VMEM((d_model // SSM_GROUP, rows // CHUNK, MXU_DIM), BF16),
                        pltpu.VMEM((coef.shape[0], batch, 2 * LANES), F32),
                        pltpu.VMEM((2, rows, d_model), BF16)],
        compiler_params=_params("arbitrary"),
        name="s5_layer",
    )(h, g[None, :], w_mat, v_mat, z_mat, coef, d_skip[None, :], w_glu)


def _conv_kernel(h_ref, g_ref, win_ref, wc_ref, wout_ref, o_ref, ext_ref, *, batch):
    d_model = h_ref.shape[1]
    rows = h_ref.shape[0]
    halo = 2 * batch

    @pl.when(pl.program_id(0) == 0)
    def _():
        ext_ref[0:halo, :] = jnp.zeros((halo, d_model), F32)

    x = h_ref[...]
    xn = _rms(x, g_ref[...]).astype(BF16)
    c_gate = _dot(xn, win_ref[:, d_model:2 * d_model])
    v = _dot(xn, win_ref[:, 2 * d_model:])
    ext_ref[halo:, :] = c_gate * v
    conv = (wc_ref[0:1, :] * ext_ref[0:rows, :]
            + wc_ref[1:2, :] * ext_ref[batch:batch + rows, :]
            + wc_ref[2:3, :] * ext_ref[halo:, :])
    ext_ref[0:halo, :] = ext_ref[rows:, :]
    b_gate = _dot(xn, win_ref[:, :d_model])
    o_ref[...] = x + _dot((b_gate * conv).astype(BF16), wout_ref[...])


def _conv_layer(h, g, w_in, w_conv, w_out, layer, *, batch, rows):
    t_rows, d_model = h.shape
    row_spec = pl.BlockSpec((rows, d_model), lambda i: (i, 0))
    return pl.pallas_call(
        functools.partial(_conv_kernel, batch=batch),
        grid=(t_rows // rows,),
        in_specs=[row_spec, _const_spec((1, d_model)), _layer_spec(w_in, layer),
                  _layer_spec(w_conv, layer), _layer_spec(w_out, layer)],
        out_specs=row_spec,
        out_shape=jax.ShapeDtypeStruct(h.shape, F32),
        scratch_shapes=[pltpu.VMEM((rows + 2 * batch, d_model), F32)],
        compiler_params=_params("arbitrary"),
        name="conv_layer",
    )(h, g[None, :], w_in, w_conv, w_out)


def _ffn_kernel(*refs, final, hidden_chunks, has_mix, row_parts):
    if has_mix:
        h_ref, mix_ref, *refs = refs
    else:
        h_ref, *refs = refs
    p_ref, gf_ref, win_ref, wout_ref, gp_ref, wgate_ref, wup_ref, gfin_ref, o_ref, hid_ref = refs
    hidden = wout_ref.shape[0]
    batch, steps, _ = p_ref.shape
    span = steps // row_parts
    times = [(k * span, (k + 1) * span) for k in range(row_parts)]

    def hidden_stage(t0, t1):
        r = slice(t0 * batch, t1 * batch)
        x = _to_time_major(h_ref[:, t0:t1, :]) if len(h_ref.shape) == 3 else h_ref[r, :]
        if has_mix:
            x = x + mix_ref[r, :]
        xn = _rms(x, gf_ref[...]).astype(BF16)
        for lo, hi in hidden_chunks:
            gate = _dot(xn, win_ref[:, lo:hi])
            up = _dot(xn, win_ref[:, hidden + lo:hidden + hi])
            hid_ref[r, lo:hi] = (gate * jax.nn.sigmoid(gate) * up).astype(BF16)
        return x

    def out_stage(t0, t1, x):
        r = slice(t0 * batch, t1 * batch)
        y = x + _dot(hid_ref[r, :], wout_ref[...])
        ple_up = _dot(_to_time_major(p_ref[:, t0:t1, :]).astype(BF16), wup_ref[...])
        return y, ple_up

    def ple_stage(t0, t1, y, ple_up):
        yn = _rms(y, gp_ref[...]).astype(BF16)
        out = y + jax.nn.sigmoid(_dot(yn, wgate_ref[...])) * ple_up
        if final:
            o_ref[:, t0:t1, :] = _to_batch_major(_rms(out, gfin_ref[...]), batch)
        else:
            o_ref[t0 * batch:t1 * batch, :] = out

    xs = [hidden_stage(*times[0])]
    pending = None
    for k in range(row_parts):
        ys = out_stage(*times[k], xs[k])
        if k + 1 < row_parts:
            xs.append(hidden_stage(*times[k + 1]))
        if pending is not None:
            ple_stage(*pending)
        pending = (*times[k], *ys)
    ple_stage(*pending)


def _chunks(total, size):
    return tuple((lo, min(lo + size, total)) for lo in range(0, total, size))


def _ffn_layer(h, mix, p, g_ffn, w_in, w_out, g_ple, w_gate, w_up, g_final, layer, *, steps, final):
    _, batch, seq, ple = p.shape
    d_model = h.shape[-1]
    t_rows = seq * batch
    hidden = w_out.shape[1]
    rows = steps * batch
    row_spec = pl.BlockSpec((rows, d_model), lambda i: (i, 0))
    batch_spec = pl.BlockSpec((batch, steps, d_model), lambda i: (0, i, 0))
    if final:
        out_spec, out_shape = batch_spec, jax.ShapeDtypeStruct((batch, seq, d_model), F32)
    else:
        out_spec, out_shape = row_spec, jax.ShapeDtypeStruct((t_rows, d_model), F32)
    acts = [h] if mix is None else [h, mix]
    act_specs = [batch_spec if h.ndim == 3 else row_spec] + ([] if mix is None else [row_spec])
    kernel = functools.partial(_ffn_kernel, final=final, has_mix=mix is not None,
                               hidden_chunks=_chunks(hidden, 4 * MXU_DIM),
                               row_parts=2 if steps % (2 * SUBLANES) == 0 else 1)
    return pl.pallas_call(
        kernel,
        grid=(t_rows // rows,),
        in_specs=act_specs + [
            pl.BlockSpec((None, batch, steps, ple), lambda i: (layer, 0, i, 0)),
            _const_spec((1, d_model)), _layer_spec(w_in, layer), _layer_spec(w_out, layer),
            _const_spec((1, d_model)), _layer_spec(w_gate, layer), _layer_spec(w_up, layer),
            _const_spec((1, d_model))],
        out_specs=out_spec,
        out_shape=out_shape,
        scratch_shapes=[pltpu.VMEM((rows, hidden), BF16)],
        compiler_params=_params("parallel"),
        name="ffn_layer",
    )(*acts, p, g_ffn[None, :], w_in, w_out, g_ple[None, :], w_gate, w_up, g_final[None, :])


def kernel(x, p, norm_mix_g, s5_a_re, s5_a_im, s5_log_dt, s5_b_re, s5_b_im, s5_c_re, s5_c_im, s5_d, s5_w_glu, conv_w_in, conv_w, conv_w_out, norm_ffn_g, ffn_w_in, ffn_w_out, norm_ple_g, ple_w_gate, ple_w_up, final_norm_g):
    batch, seq, d_model = x.shape
    depth = p.shape[0]
    assert batch == SUBLANES and d_model % MXU_DIM == 0
    steps = min(64, seq)

    s5_w_glu, conv_w_in, conv_w_out, ffn_w_in, ffn_w_out, ple_w_gate, ple_w_up = (
        w.astype(BF16) for w in (s5_w_glu, conv_w_in, conv_w_out, ffn_w_in, ffn_w_out,
                                 ple_w_gate, ple_w_up))
    h = x
    for i in range(depth):
        j = i // 2
        if i % 2 == 0:
            mats = _s5_prep(s5_a_re[j], s5_a_im[j], s5_log_dt[j], s5_b_re[j], s5_b_im[j],
                            s5_c_re[j], s5_c_im[j])
            mix = _s5_layer(h, norm_mix_g[i], *mats, s5_d[j], s5_w_glu, j,
                            batch=batch, steps=min(2 * steps, seq))
        else:
            h = _conv_layer(h, norm_mix_g[i], conv_w_in, conv_w, conv_w_out, j,
                            batch=batch, rows=steps * batch)
            mix = None
        h = _ffn_layer(h, mix, p, norm_ffn_g[i], ffn_w_in, ffn_w_out, norm_ple_g[i], ple_w_gate,
                       ple_w_up, final_norm_g, i, steps=steps, final=(i == depth - 1))
    return h
```

```python
import functools

import jax
import jax.numpy as jnp
from jax import lax
from jax.experimental import pallas as pl
from jax.experimental.pallas import tpu as pltpu

EPS = 1e-6
SSM_GROUP = 16
LANES = 128
SUBLANES = 8
MXU_DIM = 256
CHUNK = MXU_DIM // SSM_GROUP
TILE_GROUPS = LANES // SSM_GROUP
BF16_ROWS = 2 * SUBLANES
VMEM_LIMIT = 56 * 1024 * 1024

BF16 = jnp.bfloat16
F32 = jnp.float32


def _rms(x, g):
    return x * lax.rsqrt(jnp.mean(x * x, axis=-1, keepdims=True) + EPS) * g


def _dot(a, b):
    return jnp.dot(a, b, preferred_element_type=F32)


def _const_spec(shape):
    zeros = (0,) * len(shape)
    return pl.BlockSpec(shape, lambda i: zeros, pipeline_mode=pl.Buffered(1))


def _layer_spec(stacked, layer):
    tail = (0,) * (stacked.ndim - 1)
    return pl.BlockSpec((None,) + stacked.shape[1:], lambda i: (layer,) + tail,
                        pipeline_mode=pl.Buffered(1))


def _params(sem):
    return pltpu.CompilerParams(dimension_semantics=(sem,), vmem_limit_bytes=VMEM_LIMIT)


def _to_time_major(blk):
    b, s, d = blk.shape
    return jnp.swapaxes(blk, 0, 1).reshape(s * b, d)


def _to_batch_major(rows, batch):
    n, d = rows.shape
    return jnp.swapaxes(rows.reshape(n // batch, batch, d), 0, 1)


def _s5_prep_kernel(a_re_ref, a_im_ref, log_dt_ref, b_re_ref, b_im_ref, c_re_ref, c_im_ref,
                    w_ref, v_ref, z_ref, coef_ref):
    n_groups, k, p = b_re_ref.shape
    width = CHUNK * k
    first_half = lax.broadcasted_iota(jnp.int32, (1, 2 * p), 1) < p
    neg_first = jnp.where(first_half, -1.0, 1.0)
    lane_block = lax.broadcasted_iota(jnp.int32, (width, width), 1) // k

    def swap(v):
        return pltpu.roll(v, p, axis=1)

    def twice(v):
        return jnp.concatenate([v, v], axis=-1)

    for i in range(n_groups):
        a_re = a_re_ref[i]
        a_im = a_im_ref[i]
        dt = jnp.exp(log_dt_ref[i])
        mag = jnp.exp(dt * a_re)
        ang = dt * a_im
        abar_re = mag * jnp.cos(ang)
        abar_im = mag * jnp.sin(ang)
        nr = abar_re - 1.0
        ni = abar_im
        den = a_re * a_re + a_im * a_im
        f_re = (nr * a_re + ni * a_im) / den
        f_im = (ni * a_re - nr * a_im) / den
        b_re = b_re_ref[i]
        b_im = b_im_ref[i]
        bb_re = twice(f_re * b_re - f_im * b_im)
        bb_im = twice(f_re * b_im + f_im * b_re)
        c_re = c_re_ref[i]
        c_im = c_im_ref[i]

        mul_re = twice(abar_re)
        mul_im = jnp.concatenate([-abar_im, abar_im], axis=-1)
        power = jnp.where(first_half, 1.0, 0.0)
        flipped = 1.0 - power
        powers, swapped = [power], [flipped]
        for _ in range(CHUNK):
            power, flipped = (power * mul_re + flipped * mul_im,
                              flipped * mul_re - power * mul_im)
            powers.append(power)
            swapped.append(flipped)

        bp = jnp.concatenate(
            [bb_re * powers[CHUNK - 1 - rho] + bb_im * (swapped[CHUNK - 1 - rho] * neg_first)
             for rho in range(CHUNK)], axis=0)
        cc = jnp.concatenate([jnp.concatenate([c_re, -c_im], axis=-1)] * CHUNK, axis=0)
        gx = lax.dot_general(bp, cc, (((1,), (1,)), ((), ())), precision=lax.Precision.HIGHEST,
                             preferred_element_type=F32)
        w = jnp.zeros((width, width), F32)
        for t in range(CHUNK):
            up = k * (CHUNK - 1 - t)
            shifted = gx if up == 0 else jnp.concatenate(
                [gx[up:, :], jnp.zeros((up, width), F32)], axis=0)
            w = jnp.where(lane_block == t, shifted, w)
        w_ref[i] = w.astype(BF16)
        v_ref[i] = jnp.concatenate([bp, swap(bp)], axis=-1).astype(BF16)

        cr2 = twice(c_re)
        ci2 = twice(c_im)
        zt = jnp.concatenate(
            [cr2 * (powers[t + 1] * -neg_first) - ci2 * swapped[t + 1]
             for t in range(CHUNK)], axis=0)
        z_ref[i] = zt.T.astype(BF16)

        last, last_flipped = powers[CHUNK], swapped[CHUNK]
        re2 = jnp.where(first_half, last, last_flipped)
        im2 = jnp.where(first_half, last_flipped, last) * neg_first
        coef_ref[i] = jnp.concatenate(
            [re2, im2, -im2, jnp.zeros((SUBLANES - 3, 2 * p), F32)], axis=0)


def _s5_prep(a_re, a_im, log_dt, b_re, b_im, c_re, c_im):
    g, p = a_re.shape
    k = b_re.shape[-1]
    assert CHUNK * k == MXU_DIM and 2 * p == LANES
    per_step = SUBLANES
    width = CHUNK * k

    def spec(*tail):
        return pl.BlockSpec((per_step,) + tail, lambda i: (i,) + (0,) * len(tail))

    return pl.pallas_call(
        _s5_prep_kernel,
        grid=(g // per_step,),
        in_specs=[spec(1, p), spec(1, p), spec(1, 1), spec(k, p), spec(k, p), spec(k, p), spec(k, p)],
        out_specs=(spec(width, width), spec(width, 2 * LANES), spec(LANES, width), spec(SUBLANES, LANES)),
        out_shape=(jax.ShapeDtypeStruct((g, width, width), BF16),
                   jax.ShapeDtypeStruct((g, width, 2 * LANES), BF16),
                   jax.ShapeDtypeStruct((g, LANES, width), BF16),
                   jax.ShapeDtypeStruct((g, SUBLANES, LANES), F32)),
        compiler_params=_params("parallel"),
        name="s5_prep",
    )(a_re[:, None, :], a_im[:, None, :], log_dt[:, None, None], jnp.swapaxes(b_re, 1, 2),
      jnp.swapaxes(b_im, 1, 2), c_re, c_im)


def _block_transpose8(tiles, lane_block):
    v = list(tiles)
    assert len(v) == TILE_GROUPS == 8
    for d in (4, 2, 1):
        high = (lane_block & d) != 0
        for i in range(TILE_GROUPS):
            if i & d:
                continue
            lo, hi = v[i], v[i + d]
            v[i] = jnp.where(high, pltpu.roll(hi, SSM_GROUP * d, axis=1), lo)
            v[i + d] = jnp.where(high, hi, pltpu.roll(lo, LANES - SSM_GROUP * d, axis=1))
    return v


def _s5_kernel(h_ref, g_ref, w_ref, v_ref, z_ref, coef_ref, d_ref, wglu_ref,
               o_ref, u_ref, xs_ref, st_ref, act_ref, *, batch_major_in):
    rows, d_model = o_ref.shape
    n_chunks = rows // (CHUNK * SUBLANES)
    tile_groups = TILE_GROUPS
    slot = pl.program_id(0) % 2

    @pl.when(pl.program_id(0) == 0)
    def _():
        st_ref[...] = jnp.zeros_like(st_ref)
        act_ref[...] = jnp.zeros_like(act_ref)

    prev = act_ref[1 - slot]

    def glu_columns(lo):
        val = _dot(prev, wglu_ref[:, lo:lo + MXU_DIM])
        gate = _dot(prev, wglu_ref[:, d_model + lo:d_model + lo + MXU_DIM])
        o_ref[:, lo:lo + MXU_DIM] = val * jax.nn.sigmoid(gate)

    glu_columns(0)
    x = _to_time_major(h_ref[...]) if batch_major_in else h_ref[...]
    u_ref[...] = _rms(x, g_ref[...])
    lane_block = lax.broadcasted_iota(jnp.int32, (SUBLANES, LANES), 1) // SSM_GROUP
    n_tiles = d_model // LANES
    glu_every = n_tiles // (d_model // MXU_DIM)

    def step_rows(c, half, tau, n=1):
        lo = ((c * CHUNK + half * SUBLANES) + tau) * SUBLANES
        return slice(lo, lo + n * SUBLANES)

    def stage_inputs(j):
        lanes = slice(j * LANES, (j + 1) * LANES)
        for cp in range(n_chunks // 2):
            for half in range(2):
                packed = [pltpu.bitcast(jnp.concatenate(
                    [u_ref[step_rows(2 * cp, half, tau), lanes],
                     u_ref[step_rows(2 * cp + 1, half, tau), lanes]], axis=0).astype(BF16), jnp.uint32)
                    for tau in range(TILE_GROUPS)]
                moved = _block_transpose8(packed, lax.broadcasted_iota(
                    jnp.int32, packed[0].shape, 1) // SSM_GROUP)
                for gg in range(tile_groups):
                    xs_ref[j * tile_groups + gg, cp * BF16_ROWS:(cp + 1) * BF16_ROWS,
                           half * LANES:(half + 1) * LANES] = (
                        pltpu.bitcast(moved[gg], BF16))
        return [_dot(xs_ref[j * tile_groups + gg], v_ref[j * tile_groups + gg])
                for gg in range(tile_groups)]

    def chunk_starts(g, s12):
        a_re = jnp.broadcast_to(coef_ref[g, 0:1, :], (SUBLANES, LANES))
        a_im = jnp.broadcast_to(coef_ref[g, 1:2, :], (SUBLANES, LANES))
        a_im_neg = jnp.broadcast_to(coef_ref[g, 2:3, :], (SUBLANES, LANES))
        x1 = st_ref[g, :, :LANES]
        x2 = st_ref[g, :, LANES:]
        starts = []
        for c in range(n_chunks):
            starts.append(x1)
            r = slice(c * SUBLANES, (c + 1) * SUBLANES)
            x1, x2 = (a_re * x1 + a_im * x2 + s12[r, :LANES],
                      a_re * x2 + a_im_neg * x1 + s12[r, LANES:])
        st_ref[g, :, :LANES] = x1
        st_ref[g, :, LANES:] = x2
        return jnp.concatenate(starts, axis=0).astype(BF16)

    def finish_tile(j, increments):
        lanes = slice(j * LANES, (j + 1) * LANES)
        groups = range(j * tile_groups, (j + 1) * tile_groups)
        hs = [chunk_starts(g, s12) for g, s12 in zip(groups, increments)]
        ys = [_dot(xs_ref[g], w_ref[g]) + _dot(h, z_ref[g]) for g, h in zip(groups, hs)]
        d_tile = d_ref[:, lanes]
        for c in range(n_chunks):
            for half in range(2):
                r = slice(c * SUBLANES, (c + 1) * SUBLANES)
                back = _block_transpose8([y[r, half * LANES:(half + 1) * LANES] for y in ys], lane_block)
                for tau in range(0, TILE_GROUPS, 2):
                    two_steps = step_rows(c, half, tau, 2)
                    y = d_tile * u_ref[two_steps, lanes] + jnp.concatenate(back[tau:tau + 2], axis=0)
                    act_ref[slot, two_steps, lanes] = jax.nn.gelu(y).astype(BF16)

    increments = stage_inputs(0)
    for j in range(n_tiles):
        ahead = stage_inputs(j + 1) if j + 1 < n_tiles else None
        if j % glu_every == 0 and j > 0:
            glu_columns(j // glu_every * MXU_DIM)
        finish_tile(j, increments)
        increments = ahead


def _s5_layer(h, g, w_mat, v_mat, z_mat, coef, d_skip, w_glu, layer, *, batch, steps):
    batch_major_in = h.ndim == 3
    d_model = h.shape[-1]
    t_rows = h.size // d_model
    rows = steps * batch
    assert steps % (2 * CHUNK) == 0
    n_blocks = t_rows // rows
    if batch_major_in:
        in_spec = pl.BlockSpec((batch, steps, d_model), lambda i: (0, jnp.minimum(i, n_blocks - 1), 0))
    else:
        in_spec = pl.BlockSpec((rows, d_model), lambda i: (jnp.minimum(i, n_blocks - 1), 0))
    return pl.pallas_call(
        functools.partial(_s5_kernel, batch_major_in=batch_major_in),
        grid=(n_blocks + 1,),
        in_specs=[in_spec, _const_spec((1, d_model)), _const_spec(w_mat.shape),
                  _const_spec(v_mat.shape), _const_spec(z_mat.shape), _const_spec(coef.shape),
                  _const_spec((1, d_model)), _layer_spec(w_glu, layer)],
        out_specs=pl.BlockSpec((rows, d_model), lambda i: (jnp.maximum(i - 1, 0), 0)),
        out_shape=jax.ShapeDtypeStruct((t_rows, d_model), F32),
        scratch_shapes=[pltpu.VMEM((rows, d_model), F32),
                        pltpu.VMEM((d_model // SSM_GROUP, rows // CHUNK, MXU_DIM), BF16),
                        pltpu.VMEM((coef.shape[0], batch, 2 * LANES), F32),
                        pltpu.VMEM((2, rows, d_model), BF16)],
        compiler_params=_params("arbitrary"),
        name="s5_layer",
    )(h, g[None, :], w_mat, v_mat, z_mat, coef, d_skip[None, :], w_glu)


def _conv_kernel(h_ref, g_ref, win_ref, wc_ref, wout_ref, o_ref, ext_ref, *, batch):
    d_model = h_ref.shape[1]
    rows = h_ref.shape[0]
    halo = 2 * batch

    @pl.when(pl.program_id(0) == 0)
    def _():
        ext_ref[0:halo, :] = jnp.zeros((halo, d_model), F32)

    x = h_ref[...]
    xn = _rms(x, g_ref[...]).astype(BF16)
    c_gate = _dot(xn, win_ref[:, d_model:2 * d_model])
    v = _dot(xn, win_ref[:, 2 * d_model:])
    ext_ref[halo:, :] = c_gate * v
    conv = (wc_ref[0:1, :] * ext_ref[0:rows, :]
            + wc_ref[1:2, :] * ext_ref[batch:batch + rows, :]
            + wc_ref[2:3, :] * ext_ref[halo:, :])
    ext_ref[0:halo, :] = ext_ref[rows:, :]
    b_gate = _dot(xn, win_ref[:, :d_model])
    o_ref[...] = x + _dot((b_gate * conv).astype(BF16), wout_ref[...])


def _conv_layer(h, g, w_in, w_conv, w_out, layer, *, batch, rows):
    t_rows, d_model = h.shape
    row_spec = pl.BlockSpec((rows, d_model), lambda i: (i, 0))
    return pl.pallas_call(
        functools.partial(_conv_kernel, batch=batch),
        grid=(t_rows // rows,),
        in_specs=[row_spec, _const_spec((1, d_model)), _layer_spec(w_in, layer),
                  _layer_spec(w_conv, layer), _layer_spec(w_out, layer)],
        out_specs=row_spec,
        out_shape=jax.ShapeDtypeStruct(h.shape, F32),
        scratch_shapes=[pltpu.VMEM((rows + 2 * batch, d_model), F32)],
        compiler_params=_params("arbitrary"),
        name="conv_layer",
    )(h, g[None, :], w_in, w_conv, w_out)


def _ffn_kernel(*refs, final, hidden_chunks, has_mix, row_parts):
    if has_mix:
        h_ref, mix_ref, *refs = refs
    else:
        h_ref, *refs = refs
    p_ref, gf_ref, win_ref, wout_ref, gp_ref, wgate_ref, wup_ref, gfin_ref, o_ref, hid_ref = refs
    hidden = wout_ref.shape[0]
    batch, steps, _ = p_ref.shape
    span = steps // row_parts
    times = [(k * span, (k + 1) * span) for k in range(row_parts)]

    def hidden_stage(t0, t1):
        r = slice(t0 * batch, t1 * batch)
        x = _to_time_major(h_ref[:, t0:t1, :]) if len(h_ref.shape) == 3 else h_ref[r, :]
        if has_mix:
            x = x + mix_ref[r, :]
        xn = _rms(x, gf_ref[...]).astype(BF16)
        for lo, hi in hidden_chunks:
            gate = _dot(xn, win_ref[:, lo:hi])
            up = _dot(xn, win_ref[:, hidden + lo:hidden + hi])
            hid_ref[r, lo:hi] = (gate * jax.nn.sigmoid(gate) * up).astype(BF16)
        return x

    def out_stage(t0, t1, x):
        r = slice(t0 * batch, t1 * batch)
        y = x + _dot(hid_ref[r, :], wout_ref[...])
        ple_up = _dot(_to_time_major(p_ref[:, t0:t1, :]).astype(BF16), wup_ref[...])
        return y, ple_up

    def ple_stage(t0, t1, y, ple_up):
        yn = _rms(y, gp_ref[...]).astype(BF16)
        out = y + jax.nn.sigmoid(_dot(yn, wgate_ref[...])) * ple_up
        if final:
            o_ref[:, t0:t1, :] = _to_batch_major(_rms(out, gfin_ref[...]), batch)
        else:
            o_ref[t0 * batch:t1 * batch, :] = out

    xs = [hidden_stage(*times[0])]
    pending = None
    for k in range(row_parts):
        ys = out_stage(*times[k], xs[k])
        if k + 1 < row_parts:
            xs.append(hidden_stage(*times[k + 1]))
        if pending is not None:
            ple_stage(*pending)
        pending = (*times[k], *ys)
    ple_stage(*pending)


def _chunks(total, size):
    return tuple((lo, min(lo + size, total)) for lo in range(0, total, size))


def _ffn_layer(h, mix, p, g_ffn, w_in, w_out, g_ple, w_gate, w_up, g_final, layer, *, steps, final):
    _, batch, seq, ple = p.shape
    d_model = h.shape[-1]
    t_rows = seq * batch
    hidden = w_out.shape[1]
    rows = steps * batch
    row_spec = pl.BlockSpec((rows, d_model), lambda i: (i, 0))
    batch_spec = pl.BlockSpec((batch, steps, d_model), lambda i: (0, i, 0))
    if final:
        out_spec, out_shape = batch_spec, jax.ShapeDtypeStruct((batch, seq, d_model), F32)
    else:
        out_spec, out_shape = row_spec, jax.ShapeDtypeStruct((t_rows, d_model), F32)
    acts = [h] if mix is None else [h, mix]
    act_specs = [batch_spec if h.ndim == 3 else row_spec] + ([] if mix is None else [row_spec])
    kernel = functools.partial(_ffn_kernel, final=final, has_mix=mix is not None,
                               hidden_chunks=_chunks(hidden, 4 * MXU_DIM),
                               row_parts=2 if steps % (2 * SUBLANES) == 0 else 1)
    return pl.pallas_call(
        kernel,
        grid=(t_rows // rows,),
        in_specs=act_specs + [
            pl.BlockSpec((None, batch, steps, ple), lambda i: (layer, 0, i, 0)),
            _const_spec((1, d_model)), _layer_spec(w_in, layer), _layer_spec(w_out, layer),
            _const_spec((1, d_model)), _layer_spec(w_gate, layer), _layer_spec(w_up, layer),
            _const_spec((1, d_model))],
        out_specs=out_spec,
        out_shape=out_shape,
        scratch_shapes=[pltpu.VMEM((rows, hidden), BF16)],
        compiler_params=_params("parallel"),
        name="ffn_layer",
    )(*acts, p, g_ffn[None, :], w_in, w_out, g_ple[None, :], w_gate, w_up, g_final[None, :])


def kernel(x, p, norm_mix_g, s5_a_re, s5_a_im, s5_log_dt, s5_b_re, s5_b_im, s5_c_re, s5_c_im, s5_d, s5_w_glu, conv_w_in, conv_w, conv_w_out, norm_ffn_g, ffn_w_in, ffn_w_out, norm_ple_g, ple_w_gate, ple_w_up, final_norm_g):
    batch, seq, d_model = x.shape
    depth = p.shape[0]
    assert batch == SUBLANES and d_model % MXU_DIM == 0
    steps = min(64, seq)

    s5_w_glu, conv_w_in, conv_w_out, ffn_w_in, ffn_w_out, ple_w_gate, ple_w_up = (
        w.astype(BF16) for w in (s5_w_glu, conv_w_in, conv_w_out, ffn_w_in, ffn_w_out,
                                 ple_w_gate, ple_w_up))
    h = x
    for i in range(depth):
        j = i // 2
        if i % 2 == 0:
            mats = _s5_prep(s5_a_re[j], s5_a_im[j], s5_log_dt[j], s5_b_re[j], s5_b_im[j],
                            s5_c_re[j], s5_c_im[j])
            mix = _s5_layer(h, norm_mix_g[i], *mats, s5_d[j], s5_w_glu, j,
                            batch=batch, steps=min(2 * steps, seq))
        else:
            h = _conv_layer(h, norm_mix_g[i], conv_w_in, conv_w, conv_w_out, j,
                            batch=batch, rows=steps * batch)
            mix = None
        h = _ffn_layer(h, mix, p, norm_ffn_g[i], ffn_w_in, ffn_w_out, norm_ple_g[i], ple_w_gate,
                       ple_w_up, final_norm_g, i, steps=steps, final=(i == depth - 1))
    return h
```

```python
import functools

import jax
import jax.numpy as jnp
from jax import lax
from jax.experimental import pallas as pl
from jax.experimental.pallas import tpu as pltpu

EPS = 1e-6
SSM_GROUP = 16
LANES = 128
SUBLANES = 8
MXU_DIM = 256
CHUNK = MXU_DIM // SSM_GROUP
TILE_GROUPS = LANES // SSM_GROUP
BF16_ROWS = 2 * SUBLANES
VMEM_LIMIT = 56 * 1024 * 1024

BF16 = jnp.bfloat16
F32 = jnp.float32


def _rms(x, g):
    return x * lax.rsqrt(jnp.mean(x * x, axis=-1, keepdims=True) + EPS) * g


def _dot(a, b):
    return jnp.dot(a, b, preferred_element_type=F32)


def _const_spec(shape):
    zeros = (0,) * len(shape)
    return pl.BlockSpec(shape, lambda i: zeros, pipeline_mode=pl.Buffered(1))


def _layer_spec(stacked, layer):
    tail = (0,) * (stacked.ndim - 1)
    return pl.BlockSpec((None,) + stacked.shape[1:], lambda i: (layer,) + tail,
                        pipeline_mode=pl.Buffered(1))


def _params(sem, fuse_inputs=None):
    return pltpu.CompilerParams(dimension_semantics=(sem,), vmem_limit_bytes=VMEM_LIMIT,
                                allow_input_fusion=fuse_inputs)


def _to_time_major(blk):
    b, s, d = blk.shape
    return jnp.swapaxes(blk, 0, 1).reshape(s * b, d)


def _to_batch_major(rows, batch):
    n, d = rows.shape
    return jnp.swapaxes(rows.reshape(n // batch, batch, d), 0, 1)


def _s5_prep_kernel(a_re_ref, a_im_ref, log_dt_ref, b_re_ref, b_im_ref, c_re_ref, c_im_ref,
                    w_ref, v_ref, z_ref, coef_ref):
    n_groups, k, p = b_re_ref.shape
    width = CHUNK * k
    first_half = lax.broadcasted_iota(jnp.int32, (1, 2 * p), 1) < p
    neg_first = jnp.where(first_half, -1.0, 1.0)
    lane_block = lax.broadcasted_iota(jnp.int32, (width, width), 1) // k

    def swap(v):
        return pltpu.roll(v, p, axis=1)

    def twice(v):
        return jnp.concatenate([v, v], axis=-1)

    for i in range(n_groups):
        a_re = a_re_ref[i]
        a_im = a_im_ref[i]
        dt = jnp.exp(log_dt_ref[i])
        mag = jnp.exp(dt * a_re)
        ang = dt * a_im
        abar_re = mag * jnp.cos(ang)
        abar_im = mag * jnp.sin(ang)
        nr = abar_re - 1.0
        ni = abar_im
        den = a_re * a_re + a_im * a_im
        f_re = (nr * a_re + ni * a_im) / den
        f_im = (ni * a_re - nr * a_im) / den
        b_re = b_re_ref[i]
        b_im = b_im_ref[i]
        bb_re = twice(f_re * b_re - f_im * b_im)
        bb_im = twice(f_re * b_im + f_im * b_re)
        c_re = c_re_ref[i]
        c_im = c_im_ref[i]

        mul_re = twice(abar_re)
        mul_im = jnp.concatenate([-abar_im, abar_im], axis=-1)
        power = jnp.where(first_half, 1.0, 0.0)
        flipped = 1.0 - power
        powers, swapped = [power], [flipped]
        for _ in range(CHUNK):
            power, flipped = (power * mul_re + flipped * mul_im,
                              flipped * mul_re - power * mul_im)
            powers.append(power)
            swapped.append(flipped)

        bp = jnp.concatenate(
            [bb_re * powers[CHUNK - 1 - rho] + bb_im * (swapped[CHUNK - 1 - rho] * neg_first)
             for rho in range(CHUNK)], axis=0)
        cc = jnp.concatenate([jnp.concatenate([c_re, -c_im], axis=-1)] * CHUNK, axis=0)
        gx = lax.dot_general(bp, cc, (((1,), (1,)), ((), ())), precision=lax.Precision.HIGHEST,
                             preferred_element_type=F32)
        w = jnp.zeros((width, width), F32)
        for t in range(CHUNK):
            up = k * (CHUNK - 1 - t)
            shifted = gx if up == 0 else jnp.concatenate(
                [gx[up:, :], jnp.zeros((up, width), F32)], axis=0)
            w = jnp.where(lane_block == t, shifted, w)
        w_ref[i] = w.astype(BF16)
        v_ref[i] = jnp.concatenate([bp, swap(bp)], axis=-1).astype(BF16)

        cr2 = twice(c_re)
        ci2 = twice(c_im)
        zt = jnp.concatenate(
            [cr2 * (powers[t + 1] * -neg_first) - ci2 * swapped[t + 1]
             for t in range(CHUNK)], axis=0)
        z_ref[i] = zt.T.astype(BF16)

        last, last_flipped = powers[CHUNK], swapped[CHUNK]
        re2 = jnp.where(first_half, last, last_flipped)
        im2 = jnp.where(first_half, last_flipped, last) * neg_first
        coef_ref[i] = jnp.concatenate(
            [re2, im2, -im2, jnp.zeros((SUBLANES - 3, 2 * p), F32)], axis=0)


def _s5_prep(a_re, a_im, log_dt, b_re, b_im, c_re, c_im):
    g, p = a_re.shape
    k = b_re.shape[-1]
    assert CHUNK * k == MXU_DIM and 2 * p == LANES
    per_step = SUBLANES
    width = CHUNK * k

    def spec(*tail):
        return pl.BlockSpec((per_step,) + tail, lambda i: (i,) + (0,) * len(tail))

    return pl.pallas_call(
        _s5_prep_kernel,
        grid=(g // per_step,),
        in_specs=[spec(1, p), spec(1, p), spec(1, 1), spec(k, p), spec(k, p), spec(k, p), spec(k, p)],
        out_specs=(spec(width, width), spec(width, 2 * LANES), spec(LANES, width), spec(SUBLANES, LANES)),
        out_shape=(jax.ShapeDtypeStruct((g, width, width), BF16),
                   jax.ShapeDtypeStruct((g, width, 2 * LANES), BF16),
                   jax.ShapeDtypeStruct((g, LANES, width), BF16),
                   jax.ShapeDtypeStruct((g, SUBLANES, LANES), F32)),
        compiler_params=_params("parallel"),
        name="s5_prep",
    )(a_re[:, None, :], a_im[:, None, :], log_dt[:, None, None], jnp.swapaxes(b_re, 1, 2),
      jnp.swapaxes(b_im, 1, 2), c_re, c_im)


def _block_transpose8(tiles, lane_block):
    v = list(tiles)
    assert len(v) == TILE_GROUPS == 8
    for d in (4, 2, 1):
        high = (lane_block & d) != 0
        for i in range(TILE_GROUPS):
            if i & d:
                continue
            lo, hi = v[i], v[i + d]
            v[i] = jnp.where(high, pltpu.roll(hi, SSM_GROUP * d, axis=1), lo)
            v[i + d] = jnp.where(high, hi, pltpu.roll(lo, LANES - SSM_GROUP * d, axis=1))
    return v


def _s5_kernel(h_ref, g_ref, w_ref, v_ref, z_ref, coef_ref, d_ref, wglu_ref,
               o_ref, u_ref, xs_ref, st_ref, act_ref, *, batch_major_in):
    rows, d_model = o_ref.shape
    n_chunks = rows // (CHUNK * SUBLANES)
    tile_groups = TILE_GROUPS
    slot = pl.program_id(0) % 2

    @pl.when(pl.program_id(0) == 0)
    def _():
        st_ref[...] = jnp.zeros_like(st_ref)
        act_ref[...] = jnp.zeros_like(act_ref)

    prev = act_ref[1 - slot]

    def glu_columns(lo):
        val = _dot(prev, wglu_ref[:, lo:lo + MXU_DIM])
        gate = _dot(prev, wglu_ref[:, d_model + lo:d_model + lo + MXU_DIM])
        o_ref[:, lo:lo + MXU_DIM] = val * jax.nn.sigmoid(gate)

    glu_columns(0)
    x = _to_time_major(h_ref[...]) if batch_major_in else h_ref[...]
    u_ref[...] = _rms(x, g_ref[...])
    lane_block = lax.broadcasted_iota(jnp.int32, (SUBLANES, LANES), 1) // SSM_GROUP
    n_tiles = d_model // LANES
    glu_every = n_tiles // (d_model // MXU_DIM)

    def step_rows(c, half, tau, n=1):
        lo = ((c * CHUNK + half * SUBLANES) + tau) * SUBLANES
        return slice(lo, lo + n * SUBLANES)

    def stage_inputs(j):
        lanes = slice(j * LANES, (j + 1) * LANES)
        for cp in range(n_chunks // 2):
            for half in range(2):
                packed = [pltpu.bitcast(jnp.concatenate(
                    [u_ref[step_rows(2 * cp, half, tau), lanes],
                     u_ref[step_rows(2 * cp + 1, half, tau), lanes]], axis=0).astype(BF16), jnp.uint32)
                    for tau in range(TILE_GROUPS)]
                moved = _block_transpose8(packed, lax.broadcasted_iota(
                    jnp.int32, packed[0].shape, 1) // SSM_GROUP)
                for gg in range(tile_groups):
                    xs_ref[j * tile_groups + gg, cp * BF16_ROWS:(cp + 1) * BF16_ROWS,
                           half * LANES:(half + 1) * LANES] = (
                        pltpu.bitcast(moved[gg], BF16))
        return [_dot(xs_ref[j * tile_groups + gg], v_ref[j * tile_groups + gg])
                for gg in range(tile_groups)]

    def chunk_starts(g, s12):
        a_re = jnp.broadcast_to(coef_ref[g, 0:1, :], (SUBLANES, LANES))
        a_im = jnp.broadcast_to(coef_ref[g, 1:2, :], (SUBLANES, LANES))
        a_im_neg = jnp.broadcast_to(coef_ref[g, 2:3, :], (SUBLANES, LANES))
        x1 = st_ref[g, :, :LANES]
        x2 = st_ref[g, :, LANES:]
        starts = []
        for c in range(n_chunks):
            starts.append(x1)
            r = slice(c * SUBLANES, (c + 1) * SUBLANES)
            x1, x2 = (a_re * x1 + a_im * x2 + s12[r, :LANES],
                      a_re * x2 + a_im_neg * x1 + s12[r, LANES:])
        st_ref[g, :, :LANES] = x1
        st_ref[g, :, LANES:] = x2
        return jnp.concatenate(starts, axis=0).astype(BF16)

    def finish_tile(j, increments):
        lanes = slice(j * LANES, (j + 1) * LANES)
        groups = range(j * tile_groups, (j + 1) * tile_groups)
        hs = [chunk_starts(g, s12) for g, s12 in zip(groups, increments)]
        ys = [_dot(xs_ref[g], w_ref[g]) + _dot(h, z_ref[g]) for g, h in zip(groups, hs)]
        d_tile = d_ref[:, lanes]
        for c in range(n_chunks):
            for half in range(2):
                r = slice(c * SUBLANES, (c + 1) * SUBLANES)
                back = _block_transpose8([y[r, half * LANES:(half + 1) * LANES] for y in ys], lane_block)
                for tau in range(0, TILE_GROUPS, 2):
                    two_steps = step_rows(c, half, tau, 2)
                    y = d_tile * u_ref[two_steps, lanes] + jnp.concatenate(back[tau:tau + 2], axis=0)
                    act_ref[slot, two_steps, lanes] = jax.nn.gelu(y).astype(BF16)

    increments = stage_inputs(0)
    for j in range(n_tiles):
        ahead = stage_inputs(j + 1) if j + 1 < n_tiles else None
        if j % glu_every == 0 and j > 0:
            glu_columns(j // glu_every * MXU_DIM)
        finish_tile(j, increments)
        increments = ahead


def _s5_layer(h, g, w_mat, v_mat, z_mat, coef, d_skip, w_glu, layer, *, batch, steps):
    batch_major_in = h.ndim == 3
    d_model = h.shape[-1]
    t_rows = h.size // d_model
    rows = steps * batch
    assert steps % (2 * CHUNK) == 0
    n_blocks = t_rows // rows
    if batch_major_in:
        in_spec = pl.BlockSpec((batch, steps, d_model), lambda i: (0, jnp.minimum(i, n_blocks - 1), 0))
    else:
        in_spec = pl.BlockSpec((rows, d_model), lambda i: (jnp.minimum(i, n_blocks - 1), 0))
    return pl.pallas_call(
        functools.partial(_s5_kernel, batch_major_in=batch_major_in),
        grid=(n_blocks + 1,),
        in_specs=[in_spec, _const_spec((1, d_model)), _const_spec(w_mat.shape),
                  _const_spec(v_mat.shape), _const_spec(z_mat.shape), _const_spec(coef.shape),
                  _const_spec((1, d_model)), _layer_spec(w_glu, layer)],
        out_specs=pl.BlockSpec((rows, d_model), lambda i: (jnp.maximum(i - 1, 0), 0)),
        out_shape=jax.ShapeDtypeStruct((t_rows, d_model), F32),
        scratch_shapes=[pltpu.VMEM((rows, d_model), F32),
                        pltpu.VMEM((d_model // SSM_GROUP, rows // CHUNK, MXU_DIM), BF16),
                        pltpu.VMEM((coef.shape[0], batch, 2 * LANES), F32),
                        pltpu.VMEM((2, rows, d_model), BF16)],
        compiler_params=_params("arbitrary", [False] * 7 + [True]),
        name="s5_layer",
    )(h, g[None, :], w_mat, v_mat, z_mat, coef, d_skip[None, :], w_glu)


def _conv_kernel(h_ref, g_ref, win_ref, wc_ref, wout_ref, o_ref, ext_ref, *, batch):
    d_model = h_ref.shape[1]
    rows = h_ref.shape[0]
    halo = 2 * batch

    @pl.when(pl.program_id(0) == 0)
    def _():
        ext_ref[0:halo, :] = jnp.zeros((halo, d_model), F32)

    x = h_ref[...]
    xn = _rms(x, g_ref[...]).astype(BF16)
    c_gate = _dot(xn, win_ref[:, d_model:2 * d_model])
    v = _dot(xn, win_ref[:, 2 * d_model:])
    ext_ref[halo:, :] = c_gate * v
    conv = (wc_ref[0:1, :] * ext_ref[0:rows, :]
            + wc_ref[1:2, :] * ext_ref[batch:batch + rows, :]
            + wc_ref[2:3, :] * ext_ref[halo:, :])
    ext_ref[0:halo, :] = ext_ref[rows:, :]
    b_gate = _dot(xn, win_ref[:, :d_model])
    o_ref[...] = x + _dot((b_gate * conv).astype(BF16), wout_ref[...])


def _conv_layer(h, g, w_in, w_conv, w_out, layer, *, batch, rows):
    t_rows, d_model = h.shape
    row_spec = pl.BlockSpec((rows, d_model), lambda i: (i, 0))
    return pl.pallas_call(
        functools.partial(_conv_kernel, batch=batch),
        grid=(t_rows // rows,),
        in_specs=[row_spec, _const_spec((1, d_model)), _layer_spec(w_in, layer),
                  _layer_spec(w_conv, layer), _layer_spec(w_out, layer)],
        out_specs=row_spec,
        out_shape=jax.ShapeDtypeStruct(h.shape, F32),
        scratch_shapes=[pltpu.VMEM((rows + 2 * batch, d_model), F32)],
        compiler_params=_params("arbitrary", [False, False, True, False, True]),
        name="conv_layer",
    )(h, g[None, :], w_in, w_conv, w_out)


def _ffn_kernel(*refs, final, hidden_chunks, has_mix, row_parts):
    if has_mix:
        h_ref, mix_ref, *refs = refs
    else:
        h_ref, *refs = refs
    p_ref, gf_ref, win_ref, wout_ref, gp_ref, wgate_ref, wup_ref, gfin_ref, o_ref, hid_ref = refs
    hidden = wout_ref.shape[0]
    batch, steps, _ = p_ref.shape
    span = steps // row_parts
    times = [(k * span, (k + 1) * span) for k in range(row_parts)]

    def hidden_stage(t0, t1):
        r = slice(t0 * batch, t1 * batch)
        x = _to_time_major(h_ref[:, t0:t1, :]) if len(h_ref.shape) == 3 else h_ref[r, :]
        if has_mix:
            x = x + mix_ref[r, :]
        xn = _rms(x, gf_ref[...]).astype(BF16)
        for lo, hi in hidden_chunks:
            gate = _dot(xn, win_ref[:, lo:hi])
            up = _dot(xn, win_ref[:, hidden + lo:hidden + hi])
            hid_ref[r, lo:hi] = (gate * jax.nn.sigmoid(gate) * up).astype(BF16)
        return x

    def out_stage(t0, t1, x):
        r = slice(t0 * batch, t1 * batch)
        y = x + _dot(hid_ref[r, :], wout_ref[...])
        ple_up = _dot(_to_time_major(p_ref[:, t0:t1, :]).astype(BF16), wup_ref[...])
        return y, ple_up

    def ple_stage(t0, t1, y, ple_up):
        yn = _rms(y, gp_ref[...]).astype(BF16)
        out = y + jax.nn.sigmoid(_dot(yn, wgate_ref[...])) * ple_up
        if final:
            o_ref[:, t0:t1, :] = _to_batch_major(_rms(out, gfin_ref[...]), batch)
        else:
            o_ref[t0 * batch:t1 * batch, :] = out

    xs = [hidden_stage(*times[0])]
    pending = None
    for k in range(row_parts):
        ys = out_stage(*times[k], xs[k])
        if k + 1 < row_parts:
            xs.append(hidden_stage(*times[k + 1]))
        if pending is not None:
            ple_stage(*pending)
        pending = (*times[k], *ys)
    ple_stage(*pending)


def _chunks(total, size):
    return tuple((lo, min(lo + size, total)) for lo in range(0, total, size))


def _ffn_layer(h, mix, p, g_ffn, w_in, w_out, g_ple, w_gate, w_up, g_final, layer, *, steps, final):
    _, batch, seq, ple = p.shape
    d_model = h.shape[-1]
    t_rows = seq * batch
    hidden = w_out.shape[1]
    rows = steps * batch
    row_spec = pl.BlockSpec((rows, d_model), lambda i: (i, 0))
    batch_spec = pl.BlockSpec((batch, steps, d_model), lambda i: (0, i, 0))
    if final:
        out_spec, out_shape = batch_spec, jax.ShapeDtypeStruct((batch, seq, d_model), F32)
    else:
        out_spec, out_shape = row_spec, jax.ShapeDtypeStruct((t_rows, d_model), F32)
    acts = [h] if mix is None else [h, mix]
    act_specs = [batch_spec if h.ndim == 3 else row_spec] + ([] if mix is None else [row_spec])
    kernel = functools.partial(_ffn_kernel, final=final, has_mix=mix is not None,
                               hidden_chunks=_chunks(hidden, 4 * MXU_DIM),
                               row_parts=2 if steps % (2 * SUBLANES) == 0 else 1)
    return pl.pallas_call(
        kernel,
        grid=(t_rows // rows,),
        in_specs=act_specs + [
            pl.BlockSpec((None, batch, steps, ple), lambda i: (layer, 0, i, 0)),
            _const_spec((1, d_model)), _layer_spec(w_in, layer), _layer_spec(w_out, layer),
            _const_spec((1, d_model)), _layer_spec(w_gate, layer), _layer_spec(w_up, layer),
            _const_spec((1, d_model))],
        out_specs=out_spec,
        out_shape=out_shape,
        scratch_shapes=[pltpu.VMEM((rows, hidden), BF16)],
        compiler_params=_params("parallel", [False] * (len(acts) + 2)
                                + [True, True, False, True, True, False]),
        name="ffn_layer",
    )(*acts, p, g_ffn[None, :], w_in, w_out, g_ple[None, :], w_gate, w_up, g_final[None, :])


def kernel(x, p, norm_mix_g, s5_a_re, s5_a_im, s5_log_dt, s5_b_re, s5_b_im, s5_c_re, s5_c_im, s5_d, s5_w_glu, conv_w_in, conv_w, conv_w_out, norm_ffn_g, ffn_w_in, ffn_w_out, norm_ple_g, ple_w_gate, ple_w_up, final_norm_g):
    batch, seq, d_model = x.shape
    depth = p.shape[0]
    assert batch == SUBLANES and d_model % MXU_DIM == 0
    steps = min(64, seq)

    s5_w_glu, conv_w_in, conv_w_out, ffn_w_in, ffn_w_out, ple_w_gate, ple_w_up = (
        w.astype(BF16) for w in (s5_w_glu, conv_w_in, conv_w_out, ffn_w_in, ffn_w_out,
                                 ple_w_gate, ple_w_up))
    h = x
    for i in range(depth):
        j = i // 2
        if i % 2 == 0:
            mats = _s5_prep(s5_a_re[j], s5_a_im[j], s5_log_dt[j], s5_b_re[j], s5_b_im[j],
                            s5_c_re[j], s5_c_im[j])
            mix = _s5_layer(h, norm_mix_g[i], *mats, s5_d[j], s5_w_glu, j,
                            batch=batch, steps=min(2 * steps, seq))
        else:
            h = _conv_layer(h, norm_mix_g[i], conv_w_in, conv_w, conv_w_out, j,
                            batch=batch, rows=steps * batch)
            mix = None
        h = _ffn_layer(h, mix, p, norm_ffn_g[i], ffn_w_in, ffn_w_out, norm_ple_g[i], ple_w_gate,
                       ple_w_up, final_norm_g, i, steps=steps, final=(i == depth - 1))
    return h
```

```python
import functools

import jax
import jax.numpy as jnp
from jax import lax
from jax.experimental import pallas as pl
from jax.experimental.pallas import tpu as pltpu

EPS = 1e-6
SSM_GROUP = 16
LANES = 128
SUBLANES = 8
MXU_DIM = 256
CHUNK = MXU_DIM // SSM_GROUP
TILE_GROUPS = LANES // SSM_GROUP
BF16_ROWS = 2 * SUBLANES
VMEM_LIMIT = 56 * 1024 * 1024

BF16 = jnp.bfloat16
F32 = jnp.float32


def _rms(x, g):
    return x * lax.rsqrt(jnp.mean(x * x, axis=-1, keepdims=True) + EPS) * g


def _dot(a, b):
    return jnp.dot(a, b, preferred_element_type=F32)


def _const_spec(shape):
    zeros = (0,) * len(shape)
    return pl.BlockSpec(shape, lambda i: zeros, pipeline_mode=pl.Buffered(1))


def _layer_spec(stacked, layer):
    tail = (0,) * (stacked.ndim - 1)
    return pl.BlockSpec((None,) + stacked.shape[1:], lambda i: (layer,) + tail,
                        pipeline_mode=pl.Buffered(1))


def _params(sem):
    return pltpu.CompilerParams(dimension_semantics=(sem,), vmem_limit_bytes=VMEM_LIMIT)


def _to_time_major(blk):
    b, s, d = blk.shape
    return jnp.swapaxes(blk, 0, 1).reshape(s * b, d)


def _to_batch_major(rows, batch):
    n, d = rows.shape
    return jnp.swapaxes(rows.reshape(n // batch, batch, d), 0, 1)


def _s5_prep_kernel(a_re_ref, a_im_ref, log_dt_ref, b_re_ref, b_im_ref, c_re_ref, c_im_ref,
                    w_ref, v_ref, z_ref, coef_ref):
    n_groups, k, p = b_re_ref.shape
    width = CHUNK * k
    first_half = lax.broadcasted_iota(jnp.int32, (1, 2 * p), 1) < p
    neg_first = jnp.where(first_half, -1.0, 1.0)
    lane_block = lax.broadcasted_iota(jnp.int32, (width, width), 1) // k

    def swap(v):
        return pltpu.roll(v, p, axis=1)

    def twice(v):
        return jnp.concatenate([v, v], axis=-1)

    for i in range(n_groups):
        a_re = a_re_ref[i]
        a_im = a_im_ref[i]
        dt = jnp.exp(log_dt_ref[i])
        mag = jnp.exp(dt * a_re)
        ang = dt * a_im
        abar_re = mag * jnp.cos(ang)
        abar_im = mag * jnp.sin(ang)
        nr = abar_re - 1.0
        ni = abar_im
        den = a_re * a_re + a_im * a_im
        f_re = (nr * a_re + ni * a_im) / den
        f_im = (ni * a_re - nr * a_im) / den
        b_re = b_re_ref[i]
        b_im = b_im_ref[i]
        bb_re = twice(f_re * b_re - f_im * b_im)
        bb_im = twice(f_re * b_im + f_im * b_re)
        c_re = c_re_ref[i]
        c_im = c_im_ref[i]

        mul_re = twice(abar_re)
        mul_im = jnp.concatenate([-abar_im, abar_im], axis=-1)
        power = jnp.where(first_half, 1.0, 0.0)
        flipped = 1.0 - power
        powers, swapped = [power], [flipped]
        for _ in range(CHUNK):
            power, flipped = (power * mul_re + flipped * mul_im,
                              flipped * mul_re - power * mul_im)
            powers.append(power)
            swapped.append(flipped)

        bp = jnp.concatenate(
            [bb_re * powers[CHUNK - 1 - rho] + bb_im * (swapped[CHUNK - 1 - rho] * neg_first)
             for rho in range(CHUNK)], axis=0)
        cc = jnp.concatenate([jnp.concatenate([c_re, -c_im], axis=-1)] * CHUNK, axis=0)
        gx = lax.dot_general(bp, cc, (((1,), (1,)), ((), ())), precision=lax.Precision.HIGHEST,
                             preferred_element_type=F32)
        w = jnp.zeros((width, width), F32)
        for t in range(CHUNK):
            up = k * (CHUNK - 1 - t)
            shifted = gx if up == 0 else jnp.concatenate(
                [gx[up:, :], jnp.zeros((up, width), F32)], axis=0)
            w = jnp.where(lane_block == t, shifted, w)
        w_ref[i] = w.astype(BF16)
        v_ref[i] = jnp.concatenate([bp, swap(bp)], axis=-1).astype(BF16)

        cr2 = twice(c_re)
        ci2 = twice(c_im)
        zt = jnp.concatenate(
            [cr2 * (powers[t + 1] * -neg_first) - ci2 * swapped[t + 1]
             for t in range(CHUNK)], axis=0)
        z_ref[i] = zt.T.astype(BF16)

        last, last_flipped = powers[CHUNK], swapped[CHUNK]
        re2 = jnp.where(first_half, last, last_flipped)
        im2 = jnp.where(first_half, last_flipped, last) * neg_first
        coef_ref[i] = jnp.concatenate(
            [re2, im2, -im2, jnp.zeros((SUBLANES - 3, 2 * p), F32)], axis=0)


def _s5_prep(a_re, a_im, log_dt, b_re, b_im, c_re, c_im):
    g, p = a_re.shape
    k = b_re.shape[-1]
    assert CHUNK * k == MXU_DIM and 2 * p == LANES
    per_step = SUBLANES
    width = CHUNK * k

    def spec(*tail):
        return pl.BlockSpec((per_step,) + tail, lambda i: (i,) + (0,) * len(tail))

    return pl.pallas_call(
        _s5_prep_kernel,
        grid=(g // per_step,),
        in_specs=[spec(1, p), spec(1, p), spec(1, 1), spec(k, p), spec(k, p), spec(k, p), spec(k, p)],
        out_specs=(spec(width, width), spec(width, 2 * LANES), spec(LANES, width), spec(SUBLANES, LANES)),
        out_shape=(jax.ShapeDtypeStruct((g, width, width), BF16),
                   jax.ShapeDtypeStruct((g, width, 2 * LANES), BF16),
                   jax.ShapeDtypeStruct((g, LANES, width), BF16),
                   jax.ShapeDtypeStruct((g, SUBLANES, LANES), F32)),
        compiler_params=_params("parallel"),
        name="s5_prep",
    )(a_re[:, None, :], a_im[:, None, :], log_dt[:, None, None], jnp.swapaxes(b_re, 1, 2),
      jnp.swapaxes(b_im, 1, 2), c_re, c_im)


def _block_transpose8(tiles, lane_block):
    v = list(tiles)
    assert len(v) == TILE_GROUPS == 8
    for d in (4, 2, 1):
        high = (lane_block & d) != 0
        for i in range(TILE_GROUPS):
            if i & d:
                continue
            lo, hi = v[i], v[i + d]
            v[i] = jnp.where(high, pltpu.roll(hi, SSM_GROUP * d, axis=1), lo)
            v[i + d] = jnp.where(high, hi, pltpu.roll(lo, LANES - SSM_GROUP * d, axis=1))
    return v


def _s5_kernel(h_ref, g_ref, w_ref, v_ref, z_ref, coef_ref, d_ref, wglu_ref,
               o_ref, u_ref, xs_ref, st_ref, act_ref, *, batch_major_in):
    rows, d_model = o_ref.shape
    n_chunks = rows // (CHUNK * SUBLANES)
    tile_groups = TILE_GROUPS
    slot = pl.program_id(0) % 2

    @pl.when(pl.program_id(0) == 0)
    def _():
        st_ref[...] = jnp.zeros_like(st_ref)
        act_ref[...] = jnp.zeros_like(act_ref)

    prev = act_ref[1 - slot]

    def glu_columns(lo):
        val = _dot(prev, wglu_ref[:, lo:lo + MXU_DIM])
        gate = _dot(prev, wglu_ref[:, d_model + lo:d_model + lo + MXU_DIM])
        o_ref[:, lo:lo + MXU_DIM] = val * jax.nn.sigmoid(gate)

    glu_columns(0)
    x = _to_time_major(h_ref[...]) if batch_major_in else h_ref[...]
    u_ref[...] = _rms(x, g_ref[...])
    lane_block = lax.broadcasted_iota(jnp.int32, (SUBLANES, LANES), 1) // SSM_GROUP
    n_tiles = d_model // LANES
    glu_every = n_tiles // (d_model // MXU_DIM)

    def step_rows(c, half, tau, n=1):
        lo = ((c * CHUNK + half * SUBLANES) + tau) * SUBLANES
        return slice(lo, lo + n * SUBLANES)

    def stage_inputs(j):
        lanes = slice(j * LANES, (j + 1) * LANES)
        for cp in range(n_chunks // 2):
            for half in range(2):
                packed = [pltpu.bitcast(jnp.concatenate(
                    [u_ref[step_rows(2 * cp, half, tau), lanes],
                     u_ref[step_rows(2 * cp + 1, half, tau), lanes]], axis=0).astype(BF16), jnp.uint32)
                    for tau in range(TILE_GROUPS)]
                moved = _block_transpose8(packed, lax.broadcasted_iota(
                    jnp.int32, packed[0].shape, 1) // SSM_GROUP)
                for gg in range(tile_groups):
                    xs_ref[j * tile_groups + gg, cp * BF16_ROWS:(cp + 1) * BF16_ROWS,
                           half * LANES:(half + 1) * LANES] = (
                        pltpu.bitcast(moved[gg], BF16))
        return [_dot(xs_ref[j * tile_groups + gg], v_ref[j * tile_groups + gg])
                for gg in range(tile_groups)]

    def chunk_starts(g, s12):
        a_re = jnp.broadcast_to(coef_ref[g, 0:1, :], (SUBLANES, LANES))
        a_im = jnp.broadcast_to(coef_ref[g, 1:2, :], (SUBLANES, LANES))
        a_im_neg = jnp.broadcast_to(coef_ref[g, 2:3, :], (SUBLANES, LANES))
        x1 = st_ref[g, :, :LANES]
        x2 = st_ref[g, :, LANES:]
        starts = []
        for c in range(n_chunks):
            starts.append(x1)
            r = slice(c * SUBLANES, (c + 1) * SUBLANES)
            x1, x2 = (a_re * x1 + a_im * x2 + s12[r, :LANES],
                      a_re * x2 + a_im_neg * x1 + s12[r, LANES:])
        st_ref[g, :, :LANES] = x1
        st_ref[g, :, LANES:] = x2
        return jnp.concatenate(starts, axis=0).astype(BF16)

    def finish_tile(j, increments):
        lanes = slice(j * LANES, (j + 1) * LANES)
        groups = range(j * tile_groups, (j + 1) * tile_groups)
        hs = [chunk_starts(g, s12) for g, s12 in zip(groups, increments)]
        ys = [_dot(xs_ref[g], w_ref[g]) + _dot(h, z_ref[g]) for g, h in zip(groups, hs)]
        d_tile = d_ref[:, lanes]
        for c in range(n_chunks):
            for half in range(2):
                r = slice(c * SUBLANES, (c + 1) * SUBLANES)
                back = _block_transpose8([y[r, half * LANES:(half + 1) * LANES] for y in ys], lane_block)
                for tau in range(0, TILE_GROUPS, 2):
                    two_steps = step_rows(c, half, tau, 2)
                    y = d_tile * u_ref[two_steps, lanes] + jnp.concatenate(back[tau:tau + 2], axis=0)
                    act_ref[slot, two_steps, lanes] = jax.nn.gelu(y).astype(BF16)

    increments = stage_inputs(0)
    for j in range(n_tiles):
        ahead = stage_inputs(j + 1) if j + 1 < n_tiles else None
        if j % glu_every == 0 and j > 0:
            glu_columns(j // glu_every * MXU_DIM)
        finish_tile(j, increments)
        increments = ahead


def _s5_layer(h, g, w_mat, v_mat, z_mat, coef, d_skip, w_glu, layer, *, batch, steps):
    batch_major_in = h.ndim == 3
    d_model = h.shape[-1]
    t_rows = h.size // d_model
    rows = steps * batch
    assert steps % (2 * CHUNK) == 0
    n_blocks = t_rows // rows
    if batch_major_in:
        in_spec = pl.BlockSpec((batch, steps, d_model), lambda i: (0, jnp.minimum(i, n_blocks - 1), 0))
    else:
        in_spec = pl.BlockSpec((rows, d_model), lambda i: (jnp.minimum(i, n_blocks - 1), 0))
    return pl.pallas_call(
        functools.partial(_s5_kernel, batch_major_in=batch_major_in),
        grid=(n_blocks + 1,),
        in_specs=[in_spec, _const_spec((1, d_model)), _const_spec(w_mat.shape),
                  _const_spec(v_mat.shape), _const_spec(z_mat.shape), _const_spec(coef.shape),
                  _const_spec((1, d_model)), _layer_spec(w_glu, layer)],
        out_specs=pl.BlockSpec((rows, d_model), lambda i: (jnp.maximum(i - 1, 0), 0)),
        out_shape=jax.ShapeDtypeStruct((t_rows, d_model), F32),
        scratch_shapes=[pltpu.VMEM((rows, d_model), F32),
                        pltpu.VMEM((d_model // SSM_GROUP, rows // CHUNK, MXU_DIM), BF16),
                        pltpu.VMEM((coef.shape[0], batch, 2 * LANES), F32),
                        pltpu.VMEM((2, rows, d_model), BF16)],
        compiler_params=_params("arbitrary"),
        name="s5_layer",
    )(h, g[None, :], w_mat, v_mat, z_mat, coef, d_skip[None, :], w_glu)


def _conv_kernel(h_ref, g_ref, win_ref, wc_ref, wout_ref, o_ref, ext_ref, *, batch):
    d_model = h_ref.shape[1]
    rows = h_ref.shape[0]
    halo = 2 * batch

    @pl.when(pl.program_id(0) == 0)
    def _():
        ext_ref[0:halo, :] = jnp.zeros((halo, d_model), F32)

    x = h_ref[...]
    xn = _rms(x, g_ref[...]).astype(BF16)
    c_gate = _dot(xn, win_ref[:, d_model:2 * d_model])
    v = _dot(xn, win_ref[:, 2 * d_model:])
    ext_ref[halo:, :] = c_gate * v
    conv = (wc_ref[0:1, :] * ext_ref[0:rows, :]
            + wc_ref[1:2, :] * ext_ref[batch:batch + rows, :]
            + wc_ref[2:3, :] * ext_ref[halo:, :])
    ext_ref[0:halo, :] = ext_ref[rows:, :]
    b_gate = _dot(xn, win_ref[:, :d_model])
    o_ref[...] = x + _dot((b_gate * conv).astype(BF16), wout_ref[...])


def _conv_layer(h, g, w_in, w_conv, w_out, layer, *, batch, rows):
    t_rows, d_model = h.shape
    row_spec = pl.BlockSpec((rows, d_model), lambda i: (i, 0))
    return pl.pallas_call(
        functools.partial(_conv_kernel, batch=batch),
        grid=(t_rows // rows,),
        in_specs=[row_spec, _const_spec((1, d_model)), _layer_spec(w_in, layer),
                  _layer_spec(w_conv, layer), _layer_spec(w_out, layer)],
        out_specs=row_spec,
        out_shape=jax.ShapeDtypeStruct(h.shape, F32),
        scratch_shapes=[pltpu.VMEM((rows + 2 * batch, d_model), F32)],
        compiler_params=_params("arbitrary"),
        name="conv_layer",
    )(h, g[None, :], w_in, w_conv, w_out)


def _ffn_kernel(*refs, final, hidden_chunks, has_mix, row_parts):
    if has_mix:
        h_ref, mix_ref, *refs = refs
    else:
        h_ref, *refs = refs
    p_ref, gf_ref, win_ref, wout_ref, gp_ref, wgate_ref, wup_ref, gfin_ref, o_ref, hid_ref = refs
    hidden = wout_ref.shape[0]
    batch, steps, _ = p_ref.shape
    span = steps // row_parts
    times = [(k * span, (k + 1) * span) for k in range(row_parts)]

    def hidden_stage(t0, t1):
        r = slice(t0 * batch, t1 * batch)
        x = _to_time_major(h_ref[:, t0:t1, :]) if len(h_ref.shape) == 3 else h_ref[r, :]
        if has_mix:
            x = x + mix_ref[r, :]
        xn = _rms(x, gf_ref[...]).astype(BF16)
        for lo, hi in hidden_chunks:
            gate = _dot(xn, win_ref[:, lo:hi])
            up = _dot(xn, win_ref[:, hidden + lo:hidden + hi])
            hid_ref[r, lo:hi] = (gate * jax.nn.sigmoid(gate) * up).astype(BF16)
        return x

    def out_stage(t0, t1, x, ple_up):
        r = slice(t0 * batch, t1 * batch)
        return x + _dot(hid_ref[r, :], wout_ref[...]), ple_up

    def ple_stage(t0, t1, y, ple_up):
        yn = _rms(y, gp_ref[...]).astype(BF16)
        out = y + jax.nn.sigmoid(_dot(yn, wgate_ref[...])) * ple_up
        if final:
            o_ref[:, t0:t1, :] = _to_batch_major(_rms(out, gfin_ref[...]), batch)
        else:
            o_ref[t0 * batch:t1 * batch, :] = out

    ple_ups = [_dot(_to_time_major(p_ref[:, t0:t1, :]).astype(BF16), wup_ref[...]) for t0, t1 in times]
    xs = [hidden_stage(*times[0])]
    pending = None
    for k in range(row_parts):
        ys = out_stage(*times[k], xs[k], ple_ups[k])
        if k + 1 < row_parts:
            xs.append(hidden_stage(*times[k + 1]))
        if pending is not None:
            ple_stage(*pending)
        pending = (*times[k], *ys)
    ple_stage(*pending)


def _chunks(total, size):
    return tuple((lo, min(lo + size, total)) for lo in range(0, total, size))


def _ffn_layer(h, mix, p, g_ffn, w_in, w_out, g_ple, w_gate, w_up, g_final, layer, *, steps, final):
    _, batch, seq, ple = p.shape
    d_model = h.shape[-1]
    t_rows = seq * batch
    hidden = w_out.shape[1]
    rows = steps * batch
    row_spec = pl.BlockSpec((rows, d_model), lambda i: (i, 0))
    batch_spec = pl.BlockSpec((batch, steps, d_model), lambda i: (0, i, 0))
    if final:
        out_spec, out_shape = batch_spec, jax.ShapeDtypeStruct((batch, seq, d_model), F32)
    else:
        out_spec, out_shape = row_spec, jax.ShapeDtypeStruct((t_rows, d_model), F32)
    acts = [h] if mix is None else [h, mix]
    act_specs = [batch_spec if h.ndim == 3 else row_spec] + ([] if mix is None else [row_spec])
    kernel = functools.partial(_ffn_kernel, final=final, has_mix=mix is not None,
                               hidden_chunks=_chunks(hidden, 4 * MXU_DIM),
                               row_parts=2 if steps % (2 * SUBLANES) == 0 else 1)
    return pl.pallas_call(
        kernel,
        grid=(t_rows // rows,),
        in_specs=act_specs + [
            pl.BlockSpec((None, batch, steps, ple), lambda i: (layer, 0, i, 0)),
            _const_spec((1, d_model)), _layer_spec(w_in, layer), _layer_spec(w_out, layer),
            _const_spec((1, d_model)), _layer_spec(w_gate, layer), _layer_spec(w_up, layer),
            _const_spec((1, d_model))],
        out_specs=out_spec,
        out_shape=out_shape,
        scratch_shapes=[pltpu.VMEM((rows, hidden), BF16)],
        compiler_params=_params("parallel"),
        name="ffn_layer",
    )(*acts, p, g_ffn[None, :], w_in, w_out, g_ple[None, :], w_gate, w_up, g_final[None, :])


def kernel(x, p, norm_mix_g, s5_a_re, s5_a_im, s5_log_dt, s5_b_re, s5_b_im, s5_c_re, s5_c_im, s5_d, s5_w_glu, conv_w_in, conv_w, conv_w_out, norm_ffn_g, ffn_w_in, ffn_w_out, norm_ple_g, ple_w_gate, ple_w_up, final_norm_g):
    batch, seq, d_model = x.shape
    depth = p.shape[0]
    assert batch == SUBLANES and d_model % MXU_DIM == 0
    steps = min(64, seq)

    s5_w_glu, conv_w_in, conv_w_out, ffn_w_in, ffn_w_out, ple_w_gate, ple_w_up = (
        w.astype(BF16) for w in (s5_w_glu, conv_w_in, conv_w_out, ffn_w_in, ffn_w_out,
                                 ple_w_gate, ple_w_up))
    h = x
    for i in range(depth):
        j = i // 2
        if i % 2 == 0:
            mats = _s5_prep(s5_a_re[j], s5_a_im[j], s5_log_dt[j], s5_b_re[j], s5_b_im[j],
                            s5_c_re[j], s5_c_im[j])
            mix = _s5_layer(h, norm_mix_g[i], *mats, s5_d[j], s5_w_glu, j,
                            batch=batch, steps=min(2 * steps, seq))
        else:
            h = _conv_layer(h, norm_mix_g[i], conv_w_in, conv_w, conv_w_out, j,
                            batch=batch, rows=steps * batch)
            mix = None
        h = _ffn_layer(h, mix, p, norm_ffn_g[i], ffn_w_in, ffn_w_out, norm_ple_g[i], ple_w_gate,
                       ple_w_up, final_norm_g, i, steps=steps, final=(i == depth - 1))
    return h
```
